```python
import jax
import jax.numpy as jnp
from jax import lax
import numpy as np

D_MODEL = 1024
BATCH = 8
SEQ = 4096
DEPTH = 1

MLA_HEADS = 8
MLA_NOPE = 64
MLA_ROPE = 32
MLA_V = 64
MLA_Q_RANK = 384
MLA_KV_RANK = 256
ROPE_THETA = 10000.0
Q_BLOCK = 128
M_HEADS = 4
M_DK = 64
M_DV = 128
M_CHUNK = 64
CONV_W = 4
F_BIAS_LO = 3.0
F_BIAS_HI = 6.0
MLA_OUT = MLA_HEADS * MLA_V
M_OUT = M_HEADS * M_DV
D_MIX = MLA_OUT + M_OUT
D_FF = ((-(-8 * D_MODEL // 3)) + 255) // 256 * 256
IN_SPLITS = (MLA_Q_RANK, MLA_KV_RANK, MLA_ROPE, 2 * M_HEADS * M_DK, M_OUT, M_OUT, M_HEADS, M_HEADS)
D_IN = sum(IN_SPLITS)
IN_OFFSETS = tuple(int(v) for v in np.cumsum(IN_SPLITS)[:-1])
EPS = 1e-6

kernel_name = "hybrid_mla_mlstm_adaln_layer"


def rms_norm(x, g):
    xf = x.astype(jnp.float32)
    y = xf * lax.rsqrt(jnp.mean(xf * xf, axis=-1, keepdims=True) + EPS)
    return (y * g.astype(jnp.float32)).astype(x.dtype)


def head_norm(y, g, n_heads):
    b, s, w = y.shape
    yh = y.reshape(b, s, n_heads, w // n_heads)
    return rms_norm(yh, g.reshape(n_heads, w // n_heads)).reshape(b, s, w)


def apply_rope(x, positions):
    half = x.shape[-1] // 2
    inv = ROPE_THETA ** (-jnp.arange(half, dtype=jnp.float32) / half)
    ang = positions.astype(jnp.float32)[..., None] * inv
    cos = jnp.cos(ang)[:, :, None, :]
    sin = jnp.sin(ang)[:, :, None, :]
    xf = x.astype(jnp.float32)
    x1, x2 = xf[..., :half], xf[..., half:]
    return jnp.concatenate([x1 * cos - x2 * sin, x2 * cos + x1 * sin], axis=-1).astype(x.dtype)


def mla(q_lat, kv_lat, kr_lat, positions, g_q, w_uq, g_kv, w_ukv):
    b, s, _ = q_lat.shape
    q = (rms_norm(q_lat, g_q) @ w_uq).reshape(b, s, MLA_HEADS, MLA_NOPE + MLA_ROPE)
    q_nope = q[..., :MLA_NOPE]
    q_rope = apply_rope(q[..., MLA_NOPE:], positions)
    kv = (rms_norm(kv_lat, g_kv) @ w_ukv).reshape(b, s, MLA_HEADS, MLA_NOPE + MLA_V)
    k_nope = kv[..., :MLA_NOPE]
    v = kv[..., MLA_NOPE:]
    k_rope = apply_rope(kr_lat[:, :, None, :], positions)[:, :, 0, :]
    scale = (MLA_NOPE + MLA_ROPE) ** -0.5
    nb = s // Q_BLOCK
    qn_b = q_nope.reshape(b, nb, Q_BLOCK, MLA_HEADS, MLA_NOPE).transpose(1, 0, 2, 3, 4)
    qr_b = q_rope.reshape(b, nb, Q_BLOCK, MLA_HEADS, MLA_ROPE).transpose(1, 0, 2, 3, 4)
    starts = jnp.arange(nb, dtype=jnp.int32) * Q_BLOCK
    kpos = jnp.arange(s, dtype=jnp.int32)

    def block(args):
        qn, qr, start = args
        sc = (jnp.einsum('bqhd,bkhd->bhqk', qn, k_nope, preferred_element_type=jnp.float32)
              + jnp.einsum('bqhr,bkr->bhqk', qr, k_rope, preferred_element_type=jnp.float32)) * scale
        qpos = start + jnp.arange(Q_BLOCK, dtype=jnp.int32)
        causal = kpos[None, :] <= qpos[:, None]
        sc = jnp.where(causal, sc, -jnp.inf)
        p = jax.nn.softmax(sc, axis=-1).astype(v.dtype)
        return jnp.einsum('bhqk,bkhd->bqhd', p, v)

    out = lax.map(block, (qn_b, qr_b, starts))
    return out.transpose(1, 0, 2, 3, 4).reshape(b, s, MLA_HEADS * MLA_V)


def causal_conv(x, w, bias):
    ch = x.shape[-1]
    y = lax.conv_general_dilated(x, w[:, None, :].astype(x.dtype), window_strides=(1,),
                                 padding=[(CONV_W - 1, 0)], dimension_numbers=('NWC', 'WIO', 'NWC'),
                                 feature_group_count=ch)
    return y + bias


def mlstm(q, k, v, o_pre, i_pre, f_pre):
    f32 = jnp.float32
    b, s, h, _ = q.shape
    L = M_CHUNK
    nc = s // L

    def chunk(t):
        return t.reshape(b, nc, L, h, t.shape[-1]).transpose(0, 3, 1, 2, 4)

    qc = chunk(q.astype(f32))
    kc = chunk(k.astype(f32) * (M_DK ** -0.5))
    vc = chunk(v.astype(f32))
    ig = i_pre.astype(f32).reshape(b, nc, L, h).transpose(0, 3, 1, 2)
    lf = jax.nn.log_sigmoid(f_pre.astype(f32)).reshape(b, nc, L, h).transpose(0, 3, 1, 2)
    bcum = jnp.cumsum(lf, axis=-1)
    b_tot = bcum[..., -1]

    g = b_tot[..., None] - bcum + ig
    m_loc = jnp.max(g, axis=-1)
    wgt = jnp.exp(g - m_loc[..., None])
    dC = jnp.einsum('bhcl,bhclv,bhclk->bhcvk', wgt, vc, kc)
    dn = jnp.einsum('bhcl,bhclk->bhck', wgt, kc)

    def step(carry, inp):
        C, n, m = carry
        dC_c, dn_c, ml_c, bt_c = inp
        m_new = jnp.maximum(bt_c + m, ml_c)
        a = jnp.exp(bt_c + m - m_new)
        e = jnp.exp(ml_c - m_new)
        C_new = a[..., None, None] * C + e[..., None, None] * dC_c
        n_new = a[..., None] * n + e[..., None] * dn_c
        return (C_new, n_new, m_new), (C, n, m)

    init = (jnp.zeros((b, h, M_DV, M_DK), f32), jnp.zeros((b, h, M_DK), f32), jnp.zeros((b, h), f32))
    xs = (dC.transpose(2, 0, 1, 3, 4), dn.transpose(2, 0, 1, 3), m_loc.transpose(2, 0, 1), b_tot.transpose(2, 0, 1))
    _, (C0, n0, m0) = lax.scan(step, init, xs)
    C0 = C0.transpose(1, 2, 0, 3, 4)
    n0 = n0.transpose(1, 2, 0, 3)
    m0 = m0.transpose(1, 2, 0)

    causal = jnp.tril(jnp.ones((L, L), dtype=bool))
    logD = jnp.where(causal, bcum[..., :, None] - bcum[..., None, :] + ig[..., None, :], -jnp.inf)
    log_inter = bcum + m0[..., None]
    m_t = jnp.maximum(log_inter, jnp.max(logD, axis=-1))
    Dm = jnp.exp(logD - m_t[..., None])
    inter = jnp.exp(log_inter - m_t)
    sc = jnp.einsum('bhctk,bhcsk->bhcts', qc, kc) * Dm
    num = jnp.einsum('bhcts,bhcsv->bhctv', sc, vc) + inter[..., None] * jnp.einsum('bhctk,bhcvk->bhctv', qc, C0)
    den = jnp.sum(sc, axis=-1) + inter * jnp.einsum('bhctk,bhck->bhct', qc, n0)
    hh = num / jnp.maximum(jnp.abs(den), jnp.exp(-m_t))[..., None]
    hh = hh.transpose(0, 2, 3, 1, 4).reshape(b, s, h * M_DV)
    return (jax.nn.sigmoid(o_pre.astype(f32)) * hh).astype(o_pre.dtype)


def setup_inputs(seed: int = 0) -> dict:
    key = jax.random.key(seed)
    ks = jax.random.split(key, 24)
    f32 = jnp.float32

    def nrm(k, shape, scale):
        return jax.random.normal(k, shape, f32) * scale

    def gain(k, shape):
        return 1.0 + 0.02 * jax.random.normal(k, shape, f32)

    L = DEPTH
    x = nrm(ks[0], (BATCH, SEQ, D_MODEL), 1.0)
    c = nrm(ks[1], (BATCH, D_MODEL), 1.0)
    positions = jnp.broadcast_to(jnp.arange(SEQ, dtype=jnp.int32)[None, :], (BATCH, SEQ))
    w_ada = nrm(ks[2], (L, D_MODEL, 6 * D_MODEL), D_MODEL ** -0.5)
    b_ada = nrm(ks[3], (L, 6 * D_MODEL), 0.02)
    g_mix = gain(ks[4], (L, D_MODEL))
    w_in = nrm(ks[5], (L, D_MODEL, D_IN), D_MODEL ** -0.5)
    g_q = gain(ks[6], (L, MLA_Q_RANK))
    w_uq = nrm(ks[7], (L, MLA_Q_RANK, MLA_HEADS * (MLA_NOPE + MLA_ROPE)), MLA_Q_RANK ** -0.5)
    g_kv = gain(ks[8], (L, MLA_KV_RANK))
    w_ukv = nrm(ks[9], (L, MLA_KV_RANK, MLA_HEADS * (MLA_NOPE + MLA_V)), MLA_KV_RANK ** -0.5)
    conv_w = nrm(ks[10], (L, CONV_W, 2 * M_HEADS * M_DK), CONV_W ** -0.5)
    conv_b = nrm(ks[11], (L, 2 * M_HEADS * M_DK), 0.02)
    i_bias = nrm(ks[12], (L, M_HEADS), 0.1)
    f_bias = jnp.linspace(F_BIAS_LO, F_BIAS_HI, M_HEADS, dtype=f32)[None, :] + nrm(ks[13], (L, M_HEADS), 0.1)
    b_gates = jnp.concatenate([i_bias, f_bias], axis=-1)
    g_out_mla = gain(ks[14], (L, MLA_OUT))
    g_out_mlstm = gain(ks[15], (L, M_OUT))
    w_out = nrm(ks[16], (L, D_MIX, D_MODEL), D_MIX ** -0.5)
    g_ffn = gain(ks[17], (L, D_MODEL))
    w_gate = nrm(ks[18], (L, D_MODEL, D_FF), D_MODEL ** -0.5)
    w_up = nrm(ks[19], (L, D_MODEL, D_FF), D_MODEL ** -0.5)
    w_down = nrm(ks[20], (L, D_FF, D_MODEL), D_FF ** -0.5)
    g_final = gain(ks[21], (D_MODEL,))
    return {"x": x, "c": c, "positions": positions, "w_ada": w_ada, "b_ada": b_ada,
            "g_mix": g_mix, "w_in": w_in, "g_q": g_q, "w_uq": w_uq, "g_kv": g_kv, "w_ukv": w_ukv,
            "conv_w": conv_w, "conv_b": conv_b, "b_gates": b_gates,
            "g_out_mla": g_out_mla, "g_out_mlstm": g_out_mlstm, "w_out": w_out,
            "g_ffn": g_ffn, "w_gate": w_gate, "w_up": w_up, "w_down": w_down, "g_final": g_final}


def reference(x, c, positions, w_ada, b_ada, g_mix, w_in, g_q, w_uq, g_kv, w_ukv,
              conv_w, conv_b, b_gates, g_out_mla, g_out_mlstm, w_out,
              g_ffn, w_gate, w_up, w_down, g_final):
    b, s, _ = x.shape
    cond = jax.nn.silu(c)
    for l in range(DEPTH):
        mod = cond @ w_ada[l] + b_ada[l]
        sh_a, sc_a, gt_a, sh_f, sc_f, gt_f = [m[:, None, :] for m in jnp.split(mod, 6, axis=-1)]

        h = rms_norm(x, g_mix[l]) * (1.0 + sc_a) + sh_a
        z = h @ w_in[l]
        q_lat, kv_lat, kr_lat, z_qk, z_v, z_o, z_i, z_f = jnp.split(z, IN_OFFSETS, axis=-1)
        y_a = mla(q_lat, kv_lat, kr_lat, positions, g_q[l], w_uq[l], g_kv[l], w_ukv[l])
        qk = jax.nn.silu(causal_conv(z_qk, conv_w[l], conv_b[l]))
        q_m = qk[..., :M_HEADS * M_DK].reshape(b, s, M_HEADS, M_DK)
        k_m = qk[..., M_HEADS * M_DK:].reshape(b, s, M_HEADS, M_DK)
        v_m = z_v.reshape(b, s, M_HEADS, M_DV)
        y_b = mlstm(q_m, k_m, v_m, z_o, z_i + b_gates[l, :M_HEADS], z_f + b_gates[l, M_HEADS:])
        y = jnp.concatenate([head_norm(y_a, g_out_mla[l], MLA_HEADS),
                             head_norm(y_b, g_out_mlstm[l], M_HEADS)], axis=-1)
        x = x + gt_a * (y @ w_out[l])

        h = rms_norm(x, g_ffn[l]) * (1.0 + sc_f) + sh_f
        x = x + gt_f * ((jax.nn.silu(h @ w_gate[l]) * (h @ w_up[l])) @ w_down[l])
    return rms_norm(x, g_final)
```

```python
import functools

import numpy as np
import jax
import jax.numpy as jnp
from jax import lax
from jax.experimental import pallas as pl
from jax.experimental.pallas import tpu as pltpu

F32 = jnp.float32
BF16 = jnp.bfloat16

LANES = 128
SUBLANES = 8

MLA_HEADS = 8
MLA_NOPE = 64
MLA_ROPE = 32
MLA_V = 64
MLA_Q_RANK = 384
MLA_KV_RANK = 256
ROPE_THETA = 10000.0
ROPE_HALF = MLA_ROPE // 2
M_HEADS = 4
M_DK = 64
M_DV = 128
CONV_W = 4
EPS = 1e-6

MLA_OUT = MLA_HEADS * MLA_V
M_OUT = M_HEADS * M_DV
M_QK = M_HEADS * M_DK
HEAD_PAD = LANES
QK_PAD = MLA_HEADS * HEAD_PAD
ROPE_LO = MLA_NOPE
ROPE_MID = MLA_NOPE + ROPE_HALF
ROPE_HI = MLA_NOPE + MLA_ROPE

C_Q = 0
C_KV = C_Q + MLA_Q_RANK
C_QK = C_KV + MLA_KV_RANK
C_V = C_QK + 2 * M_QK
C_O = C_V + M_OUT
C_S = C_O + M_OUT
D_IN_PAD = C_S + LANES

TM_IN = 512
TQ = 256
M_CHUNK = 256
TM_OUT = 512
FF_CHUNKS = 3

VMEM_LIMIT = 56 * 1024 * 1024


def _lane_iota(shape):
    return lax.broadcasted_iota(jnp.int32, shape, len(shape) - 1)


def _row_iota(shape):
    return lax.broadcasted_iota(jnp.int32, shape, len(shape) - 2)


def _dot(a, b):
    return jnp.dot(a, b, preferred_element_type=F32)


def _dot_nt(a, b):
    return lax.dot_general(a, b, (((1,), (1,)), ((), ())), preferred_element_type=F32)


def _log_sigmoid(x):
    return jnp.minimum(x, 0.0) - jnp.log1p(jnp.exp(-jnp.abs(x)))


def _sigmoid(x):
    return 1.0 / (1.0 + jnp.exp(-x))


def _adaln_kernel(c_ref, w_ref, b_ref, o_ref):
    c = c_ref[...]
    cond = c * _sigmoid(c)
    o_ref[...] = _dot(cond, w_ref[...]) + b_ref[...]


def _adaln(c, w, b):
    bsz, d = c.shape
    n = w.shape[1]
    tn = 1536
    return pl.pallas_call(
        _adaln_kernel,
        out_shape=jax.ShapeDtypeStruct((bsz, n), F32),
        grid=(n // tn,),
        in_specs=[
            pl.BlockSpec((bsz, d), lambda j: (0, 0)),
            pl.BlockSpec((d, tn), lambda j: (0, j)),
            pl.BlockSpec((1, tn), lambda j: (0, j)),
        ],
        out_specs=pl.BlockSpec((bsz, tn), lambda j: (0, j)),
        compiler_params=pltpu.CompilerParams(
            dimension_semantics=("parallel",), vmem_limit_bytes=VMEM_LIMIT),
        name="adaln",
    )(c, w, b.reshape(1, n))


def _rope_kernel(pos_ref, inv_ref, cos_ref, sin_ref):
    ang = pos_ref[...].astype(F32) * inv_ref[0]
    cos_ref[0] = jnp.cos(ang)
    sin_ref[0] = jnp.sin(ang)


def _rope_tables(positions):
    bsz, s = positions.shape
    rows = bsz * s // LANES
    pos2d = positions.reshape(rows, LANES)
    inv = ROPE_THETA ** (-np.arange(ROPE_HALF, dtype=np.float64) / ROPE_HALF)
    inv = jnp.asarray(np.broadcast_to(inv.astype(np.float32)[:, None, None], (ROPE_HALF, 1, LANES)))
    cos_t, sin_t = pl.pallas_call(
        _rope_kernel,
        out_shape=(jax.ShapeDtypeStruct((ROPE_HALF, rows, LANES), F32),
                   jax.ShapeDtypeStruct((ROPE_HALF, rows, LANES), F32)),
        grid=(ROPE_HALF,),
        in_specs=[
            pl.BlockSpec((rows, LANES), lambda j: (0, 0)),
            pl.BlockSpec((1, 1, LANES), lambda j: (j, 0, 0)),
        ],
        out_specs=(pl.BlockSpec((1, rows, LANES), lambda j: (j, 0, 0)),
                   pl.BlockSpec((1, rows, LANES), lambda j: (j, 0, 0))),
        compiler_params=pltpu.CompilerParams(
            dimension_semantics=("parallel",), vmem_limit_bytes=VMEM_LIMIT),
        name="rope_tables",
    )(pos2d, inv)
    cos = cos_t.reshape(ROPE_HALF, bsz, s).transpose(1, 2, 0)
    sin = sin_t.reshape(ROPE_HALF, bsz, s).transpose(1, 2, 0)
    ones_lo = jnp.ones((bsz, s, ROPE_LO), F32)
    ones_hi = jnp.ones((bsz, s, HEAD_PAD - ROPE_HI), F32)
    cos_blk = jnp.concatenate([ones_lo, cos, cos, ones_hi], axis=-1)
    sin_blk = jnp.concatenate([0.0 * ones_lo, -sin, sin, 0.0 * ones_hi], axis=-1)
    return cos_blk, sin_blk


def _rms(x, g):
    return x * lax.rsqrt(jnp.mean(x * x, axis=-1, keepdims=True) + EPS) * g


def _rope_block(x, cos_blk, sin_blk, lane):
    fwd = pltpu.roll(x, HEAD_PAD - ROPE_HALF, axis=1)
    bwd = pltpu.roll(x, ROPE_HALF, axis=1)
    rot = jnp.where(lane < ROPE_MID, fwd, bwd)
    return x * cos_blk + rot * sin_blk


def _inproj_kernel(x_ref, mod_ref, gmix_ref, win_ref, gq_ref, wuq_ref, gkv_ref, wuk_ref, wv_ref,
                   convw_ref, convb_ref, bg_ref, cos_ref, sin_ref,
                   qa_ref, ka_ref, v_ref, qm_ref, kt_ref, vm_ref, og_ref, gcol_ref, grow_ref,
                   zbuf_ref):
    tm = x_ref.shape[1]
    si = pl.program_id(1)

    x = x_ref[0]
    shift = mod_ref[0, 0:1, :]
    scale = mod_ref[0, 1:2, :]
    h = _rms(x, gmix_ref[...] * (1.0 + scale)) + shift
    hb = h.astype(BF16)

    cos_blk = cos_ref[0]
    sin_blk = sin_ref[0]
    lane = _lane_iota((tm, LANES))

    ql = _dot(hb, win_ref[:, C_Q:C_KV])
    qn = _rms(ql, gq_ref[...]).astype(BF16)
    qa = _dot(qn, wuq_ref[...]) * ((MLA_NOPE + MLA_ROPE) ** -0.5)
    for hd in range(MLA_HEADS):
        blk = qa[:, hd * HEAD_PAD:(hd + 1) * HEAD_PAD]
        qa_ref[0, :, hd * HEAD_PAD:(hd + 1) * HEAD_PAD] = _rope_block(blk, cos_blk, sin_blk, lane).astype(BF16)

    zs = _dot(hb, win_ref[:, C_S:D_IN_PAD])
    kr = _rope_block(zs, cos_blk, sin_blk, lane)
    kr = jnp.where((lane >= ROPE_LO) & (lane < ROPE_HI), kr, 0.0)
    gpre = zs + bg_ref[...]
    gates = jnp.where(lane < M_HEADS, gpre, _log_sigmoid(gpre))
    gcol_ref[0] = gates
    grow_ref[0] = jnp.transpose(gates)[0:SUBLANES, :]

    kl = _dot(hb, win_ref[:, C_KV:C_QK])
    kn = _rms(kl, gkv_ref[...]).astype(BF16)
    ka = _dot(kn, wuk_ref[...])
    for hd in range(MLA_HEADS):
        ka_ref[0, :, hd * HEAD_PAD:(hd + 1) * HEAD_PAD] = (ka[:, hd * HEAD_PAD:(hd + 1) * HEAD_PAD] + kr).astype(BF16)
    v_ref[0] = _dot(kn, wv_ref[...]).astype(BF16)

    @pl.when(si == 0)
    def _():
        zbuf_ref[0:SUBLANES, :] = jnp.zeros((SUBLANES, 2 * M_QK), F32)

    zqk = _dot(hb, win_ref[:, C_QK:C_V])
    zbuf_ref[SUBLANES:SUBLANES + tm, :] = zqk
    acc = zqk * convw_ref[CONV_W - 1:CONV_W, :] + convb_ref[...]
    for j in range(1, CONV_W):
        acc = acc + zbuf_ref[SUBLANES - j:SUBLANES - j + tm, :] * convw_ref[CONV_W - 1 - j:CONV_W - j, :]
    zbuf_ref[0:SUBLANES, :] = zqk[tm - SUBLANES:tm, :]
    qk = acc * _sigmoid(acc)
    qm_ref[0] = qk[:, 0:M_QK].astype(BF16)
    kt_ref[0] = jnp.transpose(qk[:, M_QK:2 * M_QK] * (M_DK ** -0.5)).astype(BF16)

    vm_ref[0] = _dot(hb, win_ref[:, C_V:C_O]).astype(BF16)
    og_ref[0] = _sigmoid(_dot(hb, win_ref[:, C_O:C_S])).astype(BF16)


def _const_spec(shape):
    nd = len(shape)
    return pl.BlockSpec(shape, lambda *_: (0,) * nd, pipeline_mode=pl.Buffered(1))


def _inproj(x, mod, gmix, win_p, gq, wuq_p, gkv, wuk_p, wv_p, convw, convb, bg_blk, cos_blk, sin_blk):
    bsz, s, d = x.shape
    tm = TM_IN
    tok = lambda w: pl.BlockSpec((1, tm, w), lambda b, i: (b, i, 0))
    out_shapes = (
        jax.ShapeDtypeStruct((bsz, s, QK_PAD), BF16),
        jax.ShapeDtypeStruct((bsz, s, QK_PAD), BF16),
        jax.ShapeDtypeStruct((bsz, s, MLA_OUT), BF16),
        jax.ShapeDtypeStruct((bsz, s, M_QK), BF16),
        jax.ShapeDtypeStruct((bsz, M_QK, s), BF16),
        jax.ShapeDtypeStruct((bsz, s, M_OUT), BF16),
        jax.ShapeDtypeStruct((bsz, s, M_OUT), BF16),
        jax.ShapeDtypeStruct((bsz, s, LANES), F32),
        jax.ShapeDtypeStruct((bsz, SUBLANES, s), F32),
    )
    out_specs = (
        tok(QK_PAD), tok(QK_PAD), tok(MLA_OUT), tok(M_QK),
        pl.BlockSpec((1, M_QK, tm), lambda b, i: (b, 0, i)),
        tok(M_OUT), tok(M_OUT), tok(LANES),
        pl.BlockSpec((1, SUBLANES, tm), lambda b, i: (b, 0, i)),
    )
    in_specs = [
        tok(d),
        pl.BlockSpec((1, 6, d), lambda b, i: (b, 0, 0)),
        _const_spec((1, d)),
        _const_spec(win_p.shape),
        _const_spec((1, MLA_Q_RANK)),
        _const_spec(wuq_p.shape),
        _const_spec((1, MLA_KV_RANK)),
        _const_spec(wuk_p.shape),
        _const_spec(wv_p.shape),
        _const_spec((CONV_W, 2 * M_QK)),
        _const_spec((1, 2 * M_QK)),
        _const_spec((1, LANES)),
        tok(LANES), tok(LANES),
    ]
    return pl.pallas_call(
        _inproj_kernel,
        out_shape=out_shapes,
        grid=(bsz, s // tm),
        in_specs=in_specs,
        out_specs=out_specs,
        scratch_shapes=[pltpu.VMEM((tm + SUBLANES, 2 * M_QK), F32)],
        compiler_params=pltpu.CompilerParams(
            dimension_semantics=("parallel", "arbitrary"), vmem_limit_bytes=VMEM_LIMIT),
        name="inproj",
    )(x, mod, gmix, win_p, gq, wuq_p, gkv, wuk_p, wv_p, convw, convb, bg_blk, cos_blk, sin_blk)


def _mla_kernel(q_ref, k_ref, v_ref, g_ref, o_ref):
    s = q_ref.shape[1]
    nq = s // TQ
    lane = _lane_iota((TQ, LANES))
    causal = _row_iota((TQ, TQ)) >= _lane_iota((TQ, TQ))
    g = g_ref[...]

    def q_tile(qi, carry):
        q0 = pl.multiple_of(qi * TQ, TQ)
        outs = []
        for hh in range(2):
            q = q_ref[0, pl.ds(q0, TQ), hh * HEAD_PAD:(hh + 1) * HEAD_PAD]

            def kv_step(k0, m, l, acc, masked):
                k = k_ref[0, pl.ds(k0, TQ), hh * HEAD_PAD:(hh + 1) * HEAD_PAD]
                v = v_ref[0, pl.ds(k0, TQ), :]
                sc = _dot_nt(q, k)
                if masked:
                    sc = jnp.where(causal, sc, -jnp.inf)
                m_new = jnp.maximum(m, jnp.max(sc, axis=-1, keepdims=True))
                alpha = jnp.exp(m - m_new)
                p = jnp.exp(sc - m_new)
                l = alpha * l + jnp.sum(p, axis=-1, keepdims=True)
                acc = alpha * acc + _dot(p.astype(BF16), v)
                return m_new, l, acc

            def body(kj, c):
                return kv_step(pl.multiple_of(kj * TQ, TQ), *c, masked=False)

            init = (jnp.full((TQ, 1), -jnp.inf, F32), jnp.zeros((TQ, 1), F32), jnp.zeros((TQ, LANES), F32))
            m, l, acc = lax.fori_loop(0, qi, body, init)
            m, l, acc = kv_step(q0, m, l, acc, masked=True)
            outs.append(acc / l)
        o = jnp.where(lane < MLA_V, outs[0], outs[1])
        o2 = o * o
        ms0 = jnp.sum(jnp.where(lane < MLA_V, o2, 0.0), axis=-1, keepdims=True)
        ms1 = jnp.sum(jnp.where(lane < MLA_V, 0.0, o2), axis=-1, keepdims=True)
        ms = jnp.where(lane < MLA_V, ms0, ms1) * (1.0 / MLA_V)
        o_ref[0, pl.ds(q0, TQ), :] = (o * lax.rsqrt(ms + EPS) * g).astype(o_ref.dtype)
        return carry

    lax.fori_loop(0, nq, q_tile, 0)


def _mla(qa, ka, v, g_out):
    bsz, s, _ = qa.shape
    pairs = MLA_HEADS // 2
    return pl.pallas_call(
        _mla_kernel,
        out_shape=jax.ShapeDtypeStruct((bsz, s, MLA_OUT), BF16),
        grid=(bsz, pairs),
        in_specs=[
            pl.BlockSpec((1, s, 2 * HEAD_PAD), lambda b, p: (b, 0, p)),
            pl.BlockSpec((1, s, 2 * HEAD_PAD), lambda b, p: (b, 0, p)),
            pl.BlockSpec((1, s, 2 * MLA_V), lambda b, p: (b, 0, p)),
            pl.BlockSpec((1, 2 * MLA_V), lambda b, p: (0, p)),
        ],
        out_specs=pl.BlockSpec((1, s, 2 * MLA_V), lambda b, p: (b, 0, p)),
        compiler_params=pltpu.CompilerParams(
            dimension_semantics=("parallel", "parallel"), vmem_limit_bytes=VMEM_LIMIT),
        name="mla_attention",
    )(qa, ka, v, g_out)


def _split3(x):
    hi = x.astype(BF16)
    r1 = x - hi.astype(F32)
    mid = r1.astype(BF16)
    lo = (r1 - mid.astype(F32)).astype(BF16)
    return hi, mid, lo


def _mlstm_kernel(q_ref, kt_ref, v_ref, og_ref, gcol_ref, grow_ref, g_ref, o_ref, cn_ref, m_ref):
    L = q_ref.shape[1]
    ci = pl.program_id(1)

    @pl.when(ci == 0)
    def _():
        cn_ref[...] = jnp.zeros(cn_ref.shape, F32)
        m_ref[...] = jnp.zeros(m_ref.shape, F32)

    row = _row_iota((L, L))
    col = _lane_iota((L, L))
    lower = row >= col
    tri = jnp.where(lower, 1.0, 0.0).astype(BF16)
    tri_t = jnp.where(row <= col, 1.0, 0.0).astype(BF16)

    gcol = gcol_ref[0]
    grow = grow_ref[0]
    ccol = sum(_dot(tri, part) for part in _split3(gcol))
    crow = sum(_dot(part, tri_t) for part in _split3(grow))

    lane128 = _lane_iota((L, LANES))
    ones_blk = jnp.where(lane128 == 0, 1.0, 0.0).astype(BF16)
    sub128 = _row_iota((LANES, L))

    for h in range(M_HEADS):
        pr, half = divmod(h, 2)
        bc_col = ccol[:, M_HEADS + h:M_HEADS + h + 1]
        bc_row = crow[M_HEADS + h:M_HEADS + h + 1, :]
        ig_row = grow[h:h + 1, :]
        btot = bc_row[:, L - 1:L]
        m0 = m_ref[h][:, 0:1]

        qp = q_ref[0, :, pr * LANES:(pr + 1) * LANES]
        in_half = (lane128 >= half * M_DK) & (lane128 < (half + 1) * M_DK)
        qh = jnp.where(in_half, qp, jnp.zeros_like(qp))
        ktp = kt_ref[0, pr * LANES:(pr + 1) * LANES, :]
        vaug = jnp.concatenate([v_ref[0, :, h * M_DV:(h + 1) * M_DV], ones_blk], axis=1)

        log_d = jnp.where(lower, bc_col - bc_row + ig_row, -jnp.inf)
        m_d = jnp.max(log_d, axis=-1, keepdims=True)
        log_inter = bc_col + m0
        m_t = jnp.maximum(log_inter, m_d)
        d_m = jnp.exp(log_d - m_t)
        inter = jnp.exp(log_inter - m_t)
        sc = (_dot(qh, ktp) * d_m).astype(BF16)
        intra = _dot(sc, vaug)
        cn = cn_ref[h]
        cross = _dot(qh, cn.astype(BF16))
        num = intra[:, 0:M_DV] + inter * cross[:, 0:M_DV]
        den = intra[:, M_DV:M_DV + 1] + inter * cross[:, M_DV:M_DV + 1]
        hh = num / jnp.maximum(jnp.abs(den), jnp.exp(-m_t))
        y = og_ref[0, :, h * M_DV:(h + 1) * M_DV].astype(F32) * hh
        y = _rms(y, g_ref[:, h * M_DV:(h + 1) * M_DV])
        o_ref[0, :, h * M_DV:(h + 1) * M_DV] = y.astype(o_ref.dtype)

        g_row = btot - bc_row + ig_row
        m_loc = jnp.max(g_row, axis=-1, keepdims=True)
        wgt = jnp.exp(g_row - m_loc)
        in_rows = (sub128 >= half * M_DK) & (sub128 < (half + 1) * M_DK)
        ktw = jnp.where(in_rows, ktp.astype(F32) * wgt, 0.0).astype(BF16)
        d_cn = _dot(ktw, vaug)
        m_new = jnp.maximum(btot + m0, m_loc)
        a = jnp.exp(btot + m0 - m_new)
        e = jnp.exp(m_loc - m_new)
        cn_ref[h] = a * cn + e * d_cn
        m_ref[h] = jnp.broadcast_to(m_new, (1, LANES))


def _mlstm(qm, kt, vm, og, gcol, grow, g_out):
    bsz, s, _ = qm.shape
    L = M_CHUNK
    tok = lambda w: pl.BlockSpec((1, L, w), lambda b, c: (b, c, 0))
    return pl.pallas_call(
        _mlstm_kernel,
        out_shape=jax.ShapeDtypeStruct((bsz, s, M_OUT), BF16),
        grid=(bsz, s // L),
        in_specs=[
            tok(M_QK),
            pl.BlockSpec((1, M_QK, L), lambda b, c: (b, 0, c)),
            tok(M_OUT), tok(M_OUT), tok(LANES),
            pl.BlockSpec((1, SUBLANES, L), lambda b, c: (b, 0, c)),
            pl.BlockSpec((1, M_OUT), lambda b, c: (0, 0)),
        ],
        out_specs=tok(M_OUT),
        scratch_shapes=[pltpu.VMEM((M_HEADS, LANES, 2 * M_DV), F32),
                        pltpu.VMEM((M_HEADS, 1, LANES), F32)],
        compiler_params=pltpu.CompilerParams(
            dimension_semantics=("parallel", "arbitrary"), vmem_limit_bytes=VMEM_LIMIT),
        name="mlstm",
    )(qm, kt, vm, og, gcol, grow, g_out)


def _outffn_kernel(x_ref, ya_ref, yb_ref, mod_ref, wout_ref, gffn_ref, wg_ref, wu_ref, wd_ref, gfin_ref,
                   o_ref, *, final_norm):
    x = x_ref[0]
    gate_a = mod_ref[0, 2:3, :]
    shift_f = mod_ref[0, 3:4, :]
    scale_f = mod_ref[0, 4:5, :]
    gate_f = mod_ref[0, 5:6, :]

    mix = _dot(ya_ref[0], wout_ref[0:MLA_OUT, :]) + _dot(yb_ref[0], wout_ref[MLA_OUT:MLA_OUT + M_OUT, :])
    x1 = x + gate_a * mix
    hb = (_rms(x1, gffn_ref[...] * (1.0 + scale_f)) + shift_f).astype(BF16)

    d_ff = wg_ref.shape[1]
    step = -(-d_ff // FF_CHUNKS // (2 * LANES)) * (2 * LANES)
    ffn = None
    for c0 in range(0, d_ff, step):
        c1 = min(c0 + step, d_ff)
        gt = _dot(hb, wg_ref[:, c0:c1])
        up = _dot(hb, wu_ref[:, c0:c1])
        act = (gt * _sigmoid(gt) * up).astype(BF16)
        part = _dot(act, wd_ref[c0:c1, :])
        ffn = part if ffn is None else ffn + part
    x2 = x1 + gate_f * ffn
    if final_norm:
        x2 = _rms(x2, gfin_ref[...])
    o_ref[0] = x2


def _outffn(x, ya, yb, mod, wout, gffn, wg, wu, wd, gfin, final_norm):
    bsz, s, d = x.shape
    tm = TM_OUT
    tok = lambda w: pl.BlockSpec((1, tm, w), lambda b, i: (b, i, 0))
    return pl.pallas_call(
        functools.partial(_outffn_kernel, final_norm=final_norm),
        out_shape=jax.ShapeDtypeStruct((bsz, s, d), F32),
        grid=(bsz, s // tm),
        in_specs=[
            tok(d), tok(MLA_OUT), tok(M_OUT),
            pl.BlockSpec((1, 6, d), lambda b, i: (b, 0, 0)),
            _const_spec(wout.shape),
            _const_spec((1, d)),
            _const_spec(wg.shape), _const_spec(wu.shape), _const_spec(wd.shape),
            _const_spec((1, d)),
        ],
        out_specs=tok(d),
        compiler_params=pltpu.CompilerParams(
            dimension_semantics=("parallel", "parallel"), vmem_limit_bytes=VMEM_LIMIT),
        name="outproj_ffn",
    )(x, ya, yb, mod, wout, gffn, wg, wu, wd, gfin)


def _prep_w_in(w_in):
    d = w_in.shape[0]
    o_q, o_kv, o_kr = 0, MLA_Q_RANK, MLA_Q_RANK + MLA_KV_RANK
    o_qk = o_kr + MLA_ROPE
    o_v = o_qk + 2 * M_QK
    o_o = o_v + M_OUT
    o_i = o_o + M_OUT
    o_f = o_i + M_HEADS
    small = jnp.concatenate([
        w_in[:, o_i:o_i + M_HEADS], w_in[:, o_f:o_f + M_HEADS],
        jnp.zeros((d, ROPE_LO - 2 * M_HEADS), w_in.dtype),
        w_in[:, o_kr:o_kr + MLA_ROPE],
        jnp.zeros((d, HEAD_PAD - ROPE_HI), w_in.dtype)], axis=1)
    return jnp.concatenate([w_in[:, o_q:o_kr], w_in[:, o_qk:o_i], small], axis=1).astype(BF16)


def _prep_w_uq(w_uq):
    r = w_uq.shape[0]
    w = w_uq.reshape(r, MLA_HEADS, MLA_NOPE + MLA_ROPE)
    w = jnp.pad(w, ((0, 0), (0, 0), (0, HEAD_PAD - MLA_NOPE - MLA_ROPE)))
    return w.reshape(r, QK_PAD).astype(BF16)


def _prep_w_ukv(w_ukv):
    r = w_ukv.shape[0]
    w = w_ukv.reshape(r, MLA_HEADS, MLA_NOPE + MLA_V)
    wk = jnp.pad(w[:, :, :MLA_NOPE], ((0, 0), (0, 0), (0, HEAD_PAD - MLA_NOPE))).reshape(r, QK_PAD)
    wv = w[:, :, MLA_NOPE:].reshape(r, MLA_OUT)
    return wk.astype(BF16), wv.astype(BF16)


def kernel(x, c, positions, w_ada, b_ada, g_mix, w_in, g_q, w_uq, g_kv, w_ukv, conv_w, conv_b, b_gates,
           g_out_mla, g_out_mlstm, w_out, g_ffn, w_gate, w_up, w_down, g_final):
    bsz, s, d = x.shape
    depth = w_ada.shape[0]
    cos_blk, sin_blk = _rope_tables(positions)
    row = lambda v: v.reshape(1, -1)
    for l in range(depth):
        mod = _adaln(c, w_ada[l], b_ada[l]).reshape(bsz, 6, d)
        wuk_p, wv_p = _prep_w_ukv(w_ukv[l])
        bg_blk = jnp.pad(b_gates[l], (0, LANES - 2 * M_HEADS)).reshape(1, LANES)
        qa, ka, v, qm, kt, vm, og, gcol, grow = _inproj(
            x, mod, row(g_mix[l]), _prep_w_in(w_in[l]), row(g_q[l]), _prep_w_uq(w_uq[l]),
            row(g_kv[l]), wuk_p, wv_p, conv_w[l], row(conv_b[l]), bg_blk, cos_blk, sin_blk)
        ya = _mla(qa, ka, v, row(g_out_mla[l]))
        yb = _mlstm(qm, kt, vm, og, gcol, grow, row(g_out_mlstm[l]))
        x = _outffn(x, ya, yb, mod, w_out[l].astype(BF16), row(g_ffn[l]),
                    w_gate[l].astype(BF16), w_up[l].astype(BF16), w_down[l].astype(BF16),
                    row(g_final), final_norm=(l == depth - 1))
    return x
```

```python
import functools

import numpy as np
import jax
import jax.numpy as jnp
from jax import lax
from jax.experimental import pallas as pl
from jax.experimental.pallas import tpu as pltpu

F32 = jnp.float32
BF16 = jnp.bfloat16

LANES = 128
SUBLANES = 8

MLA_HEADS = 8
MLA_NOPE = 64
MLA_ROPE = 32
MLA_V = 64
MLA_Q_RANK = 384
MLA_KV_RANK = 256
ROPE_THETA = 10000.0
ROPE_HALF = MLA_ROPE // 2
M_HEADS = 4
M_DK = 64
M_DV = 128
CONV_W = 4
EPS = 1e-6
LOG2E = 1.4426950408889634

MLA_OUT = MLA_HEADS * MLA_V
M_OUT = M_HEADS * M_DV
M_QK = M_HEADS * M_DK
HEAD_PAD = LANES
QK_PAD = MLA_HEADS * HEAD_PAD
ROPE_LO = MLA_NOPE
ROPE_MID = MLA_NOPE + ROPE_HALF
ROPE_HI = MLA_NOPE + MLA_ROPE

C_Q = 0
C_KV = C_Q + MLA_Q_RANK
C_QK = C_KV + MLA_KV_RANK
C_V = C_QK + 2 * M_QK
C_O = C_V + M_OUT
C_S = C_O + M_OUT
D_IN_PAD = C_S + LANES

TM_IN = 512
TQ = 512
MLA_HPS = 4
ONES_ROWS = 16
M_CHUNK = 256
TM_OUT = 512
FF_CHUNKS = 3

VMEM_LIMIT = 56 * 1024 * 1024


def _lane_iota(shape):
    return lax.broadcasted_iota(jnp.int32, shape, len(shape) - 1)


def _row_iota(shape):
    return lax.broadcasted_iota(jnp.int32, shape, len(shape) - 2)


def _dot(a, b):
    return jnp.dot(a, b, preferred_element_type=F32)


def _dot_nt(a, b):
    return lax.dot_general(a, b, (((1,), (1,)), ((), ())), preferred_element_type=F32)


def _log_sigmoid(x):
    return jnp.minimum(x, 0.0) - jnp.log1p(jnp.exp(-jnp.abs(x)))


def _sigmoid(x):
    return 1.0 / (1.0 + jnp.exp(-x))


def _adaln_kernel(c_ref, w_ref, b_ref, o_ref):
    c = c_ref[...]
    cond = c * _sigmoid(c)
    o_ref[...] = _dot(cond, w_ref[...]) + b_ref[...]


def _adaln(c, w, b):
    bsz, d = c.shape
    n = w.shape[1]
    tn = 1536
    return pl.pallas_call(
        _adaln_kernel,
        out_shape=jax.ShapeDtypeStruct((bsz, n), F32),
        grid=(n // tn,),
        in_specs=[
            pl.BlockSpec((bsz, d), lambda j: (0, 0)),
            pl.BlockSpec((d, tn), lambda j: (0, j)),
            pl.BlockSpec((1, tn), lambda j: (0, j)),
        ],
        out_specs=pl.BlockSpec((bsz, tn), lambda j: (0, j)),
        compiler_params=pltpu.CompilerParams(
            dimension_semantics=("parallel",), vmem_limit_bytes=VMEM_LIMIT),
        name="adaln",
    )(c, w, b.reshape(1, n))


def _rope_kernel(pos_ref, inv_ref, cos_ref, sin_ref):
    ang = pos_ref[...].astype(F32) * inv_ref[0]
    cos_ref[0] = jnp.cos(ang)
    sin_ref[0] = jnp.sin(ang)


def _rope_tables(positions):
    bsz, s = positions.shape
    rows = bsz * s // LANES
    pos2d = positions.reshape(rows, LANES)
    inv = ROPE_THETA ** (-np.arange(ROPE_HALF, dtype=np.float64) / ROPE_HALF)
    inv = jnp.asarray(np.broadcast_to(inv.astype(np.float32)[:, None, None], (ROPE_HALF, 1, LANES)))
    cos_t, sin_t = pl.pallas_call(
        _rope_kernel,
        out_shape=(jax.ShapeDtypeStruct((ROPE_HALF, rows, LANES), F32),
                   jax.ShapeDtypeStruct((ROPE_HALF, rows, LANES), F32)),
        grid=(ROPE_HALF,),
        in_specs=[
            pl.BlockSpec((rows, LANES), lambda j: (0, 0)),
            pl.BlockSpec((1, 1, LANES), lambda j: (j, 0, 0)),
        ],
        out_specs=(pl.BlockSpec((1, rows, LANES), lambda j: (j, 0, 0)),
                   pl.BlockSpec((1, rows, LANES), lambda j: (j, 0, 0))),
        compiler_params=pltpu.CompilerParams(
            dimension_semantics=("parallel",), vmem_limit_bytes=VMEM_LIMIT),
        name="rope_tables",
    )(pos2d, inv)
    cos = cos_t.reshape(ROPE_HALF, bsz, s).transpose(1, 2, 0)
    sin = sin_t.reshape(ROPE_HALF, bsz, s).transpose(1, 2, 0)
    ones_lo = jnp.ones((bsz, s, ROPE_LO), F32)
    ones_hi = jnp.ones((bsz, s, HEAD_PAD - ROPE_HI), F32)
    cos_blk = jnp.concatenate([ones_lo, cos, cos, ones_hi], axis=-1)
    sin_blk = jnp.concatenate([0.0 * ones_lo, -sin, sin, 0.0 * ones_hi], axis=-1)
    return cos_blk, sin_blk


def _rms(x, g):
    return x * lax.rsqrt(jnp.mean(x * x, axis=-1, keepdims=True) + EPS) * g


def _rope_block(x, cos_blk, sin_blk, lane):
    fwd = pltpu.roll(x, HEAD_PAD - ROPE_HALF, axis=1)
    bwd = pltpu.roll(x, ROPE_HALF, axis=1)
    rot = jnp.where(lane < ROPE_MID, fwd, bwd)
    return x * cos_blk + rot * sin_blk


def _inproj_kernel(x_ref, mod_ref, gmix_ref, win_ref, gq_ref, wuq_ref, gkv_ref, wuk_ref, wv_ref,
                   convw_ref, convb_ref, bg_ref, cos_ref, sin_ref,
                   qa_ref, ka_ref, vt_ref, qm_ref, kt_ref, vm_ref, og_ref, gcol_ref, grow_ref,
                   zbuf_ref):
    tm = x_ref.shape[1]
    si = pl.program_id(1)

    x = x_ref[0]
    shift = mod_ref[0, 0:1, :]
    scale = mod_ref[0, 1:2, :]
    h = _rms(x, gmix_ref[...] * (1.0 + scale)) + shift
    hb = h.astype(BF16)

    cos_blk = cos_ref[0]
    sin_blk = sin_ref[0]
    lane = _lane_iota((tm, LANES))

    ql = _dot(hb, win_ref[:, C_Q:C_KV])
    qn = _rms(ql, gq_ref[...]).astype(BF16)
    qa = _dot(qn, wuq_ref[...]) * ((MLA_NOPE + MLA_ROPE) ** -0.5 * LOG2E)
    for hd in range(MLA_HEADS):
        blk = qa[:, hd * HEAD_PAD:(hd + 1) * HEAD_PAD]
        qa_ref[0, :, hd * HEAD_PAD:(hd + 1) * HEAD_PAD] = _rope_block(blk, cos_blk, sin_blk, lane).astype(BF16)

    zs = _dot(hb, win_ref[:, C_S:D_IN_PAD])
    kr = _rope_block(zs, cos_blk, sin_blk, lane)
    kr = jnp.where((lane >= ROPE_LO) & (lane < ROPE_HI), kr, 0.0)
    gpre = zs + bg_ref[...]
    gates = jnp.where(lane < M_HEADS, gpre, _log_sigmoid(gpre))
    gcol_ref[0] = gates
    grow_ref[0] = jnp.transpose(gates)[0:SUBLANES, :]

    kl = _dot(hb, win_ref[:, C_KV:C_QK])
    kn = _rms(kl, gkv_ref[...]).astype(BF16)
    ka = _dot(kn, wuk_ref[...])
    for hd in range(MLA_HEADS):
        ka_ref[0, :, hd * HEAD_PAD:(hd + 1) * HEAD_PAD] = (ka[:, hd * HEAD_PAD:(hd + 1) * HEAD_PAD] + kr).astype(BF16)
    vt_ref[0] = jnp.transpose(_dot(kn, wv_ref[...])).astype(BF16)

    @pl.when(si == 0)
    def _():
        zbuf_ref[0:SUBLANES, :] = jnp.zeros((SUBLANES, 2 * M_QK), F32)

    zqk = _dot(hb, win_ref[:, C_QK:C_V])
    zbuf_ref[SUBLANES:SUBLANES + tm, :] = zqk
    acc = zqk * convw_ref[CONV_W - 1:CONV_W, :] + convb_ref[...]
    for j in range(1, CONV_W):
        acc = acc + zbuf_ref[SUBLANES - j:SUBLANES - j + tm, :] * convw_ref[CONV_W - 1 - j:CONV_W - j, :]
    zbuf_ref[0:SUBLANES, :] = zqk[tm - SUBLANES:tm, :]
    qk = acc * _sigmoid(acc)
    qm_ref[0] = qk[:, 0:M_QK].astype(BF16)
    kt_ref[0] = jnp.transpose(qk[:, M_QK:2 * M_QK] * (M_DK ** -0.5)).astype(BF16)

    vm_ref[0] = _dot(hb, win_ref[:, C_V:C_O]).astype(BF16)
    og_ref[0] = _sigmoid(_dot(hb, win_ref[:, C_O:C_S])).astype(BF16)


def _const_spec(shape):
    nd = len(shape)
    return pl.BlockSpec(shape, lambda *_: (0,) * nd, pipeline_mode=pl.Buffered(1))


def _inproj(x, mod, gmix, win_p, gq, wuq_p, gkv, wuk_p, wv_p, convw, convb, bg_blk, cos_blk, sin_blk):
    bsz, s, d = x.shape
    tm = TM_IN
    tok = lambda w: pl.BlockSpec((1, tm, w), lambda b, i: (b, i, 0))
    out_shapes = (
        jax.ShapeDtypeStruct((bsz, s, QK_PAD), BF16),
        jax.ShapeDtypeStruct((bsz, s, QK_PAD), BF16),
        jax.ShapeDtypeStruct((bsz, MLA_OUT, s), BF16),
        jax.ShapeDtypeStruct((bsz, s, M_QK), BF16),
        jax.ShapeDtypeStruct((bsz, M_QK, s), BF16),
        jax.ShapeDtypeStruct((bsz, s, M_OUT), BF16),
        jax.ShapeDtypeStruct((bsz, s, M_OUT), BF16),
        jax.ShapeDtypeStruct((bsz, s, LANES), F32),
        jax.ShapeDtypeStruct((bsz, SUBLANES, s), F32),
    )
    out_specs = (
        tok(QK_PAD), tok(QK_PAD),
        pl.BlockSpec((1, MLA_OUT, tm), lambda b, i: (b, 0, i)),
        tok(M_QK),
        pl.BlockSpec((1, M_QK, tm), lambda b, i: (b, 0, i)),
        tok(M_OUT), tok(M_OUT), tok(LANES),
        pl.BlockSpec((1, SUBLANES, tm), lambda b, i: (b, 0, i)),
    )
    in_specs = [
        tok(d),
        pl.BlockSpec((1, 6, d), lambda b, i: (b, 0, 0)),
        _const_spec((1, d)),
        _const_spec(win_p.shape),
        _const_spec((1, MLA_Q_RANK)),
        _const_spec(wuq_p.shape),
        _const_spec((1, MLA_KV_RANK)),
        _const_spec(wuk_p.shape),
        _const_spec(wv_p.shape),
        _const_spec((CONV_W, 2 * M_QK)),
        _const_spec((1, 2 * M_QK)),
        _const_spec((1, LANES)),
        tok(LANES), tok(LANES),
    ]
    return pl.pallas_call(
        _inproj_kernel,
        out_shape=out_shapes,
        grid=(bsz, s // tm),
        in_specs=in_specs,
        out_specs=out_specs,
        scratch_shapes=[pltpu.VMEM((tm + SUBLANES, 2 * M_QK), F32)],
        compiler_params=pltpu.CompilerParams(
            dimension_semantics=("parallel", "arbitrary"), vmem_limit_bytes=VMEM_LIMIT),
        name="inproj",
    )(x, mod, gmix, win_p, gq, wuq_p, gkv, wuk_p, wv_p, convw, convb, bg_blk, cos_blk, sin_blk)


def _mla_kernel(q_ref, k_ref, vt_ref, g_ref, o_ref):
    s = q_ref.shape[1]
    nq = s // TQ
    causal_t = _row_iota((TQ, TQ)) <= _lane_iota((TQ, TQ))
    ones_rows = jnp.ones((ONES_ROWS, TQ), BF16)

    def q_tile(qi, carry):
        q0 = pl.multiple_of(qi * TQ, TQ)
        qs = [q_ref[0, pl.ds(q0, TQ), h * HEAD_PAD:(h + 1) * HEAD_PAD] for h in range(MLA_HPS)]

        def scores(k0):
            return tuple(_dot_nt(k_ref[0, pl.ds(k0, TQ), h * HEAD_PAD:(h + 1) * HEAD_PAD], qs[h])
                         for h in range(MLA_HPS))

        def kv_step(k0, sts, state, masked):
            soft, pvs = [], []
            for h in range(MLA_HPS):
                m = state[h][0]
                st = jnp.where(causal_t, sts[h], -jnp.inf) if masked else sts[h]
                m_new = jnp.maximum(m, jnp.max(st, axis=0, keepdims=True))
                soft.append((m_new, jnp.exp2(m - m_new)))
                p = jnp.exp2(st - m_new).astype(BF16)
                vt = jnp.concatenate([vt_ref[0, h * MLA_V:(h + 1) * MLA_V, pl.ds(k0, TQ)], ones_rows], axis=0)
                pvs.append(_dot(vt, p))
            return tuple((soft[h][0], soft[h][1] * state[h][1] + pvs[h]) for h in range(MLA_HPS))

        def body(kj, state):
            k0 = pl.multiple_of(kj * TQ, TQ)
            return kv_step(k0, scores(k0), state, False)

        init = tuple((jnp.full((1, TQ), -jnp.inf, F32), jnp.zeros((MLA_V + ONES_ROWS, TQ), F32))
                     for _ in range(MLA_HPS))
        state = lax.fori_loop(0, qi, body, init)
        state = kv_step(q0, scores(q0), state, True)

        for pr in range(MLA_HPS // 2):
            halves = []
            for h in (2 * pr, 2 * pr + 1):
                acc = state[h][1]
                o = acc[0:MLA_V] / acc[MLA_V:MLA_V + 1]
                ms = jnp.mean(o * o, axis=0, keepdims=True)
                halves.append(o * lax.rsqrt(ms + EPS))
            y = jnp.transpose(jnp.concatenate(halves, axis=0))
            y = y * g_ref[:, pr * LANES:(pr + 1) * LANES]
            o_ref[0, pl.ds(q0, TQ), pr * LANES:(pr + 1) * LANES] = y.astype(o_ref.dtype)
        return carry

    lax.fori_loop(0, nq, q_tile, 0)


def _mla(qa, ka, vt, g_out):
    bsz, s, _ = qa.shape
    groups = MLA_HEADS // MLA_HPS
    return pl.pallas_call(
        _mla_kernel,
        out_shape=jax.ShapeDtypeStruct((bsz, s, MLA_OUT), BF16),
        grid=(bsz, groups),
        in_specs=[
            pl.BlockSpec((1, s, MLA_HPS * HEAD_PAD), lambda b, p: (b, 0, p)),
            pl.BlockSpec((1, s, MLA_HPS * HEAD_PAD), lambda b, p: (b, 0, p)),
            pl.BlockSpec((1, MLA_HPS * MLA_V, s), lambda b, p: (b, p, 0)),
            pl.BlockSpec((1, MLA_HPS * MLA_V), lambda b, p: (0, p)),
        ],
        out_specs=pl.BlockSpec((1, s, MLA_HPS * MLA_V), lambda b, p: (b, 0, p)),
        compiler_params=pltpu.CompilerParams(
            dimension_semantics=("parallel", "parallel"), vmem_limit_bytes=VMEM_LIMIT),
        name="mla_attention",
    )(qa, ka, vt, g_out)


def _split3(x):
    hi = x.astype(BF16)
    r1 = x - hi.astype(F32)
    mid = r1.astype(BF16)
    lo = (r1 - mid.astype(F32)).astype(BF16)
    return hi, mid, lo


def _mlstm_kernel(q_ref, kt_ref, v_ref, og_ref, gcol_ref, grow_ref, g_ref, o_ref, cn_ref, m_ref):
    L = q_ref.shape[1]
    ci = pl.program_id(1)

    @pl.when(ci == 0)
    def _():
        cn_ref[...] = jnp.zeros(cn_ref.shape, F32)
        m_ref[...] = jnp.zeros(m_ref.shape, F32)

    row = _row_iota((L, L))
    col = _lane_iota((L, L))
    lower = row >= col
    tri = jnp.where(lower, 1.0, 0.0).astype(BF16)
    tri_t = jnp.where(row <= col, 1.0, 0.0).astype(BF16)

    gcol = gcol_ref[0]
    grow = grow_ref[0]
    ccol = sum(_dot(tri, part) for part in _split3(gcol))
    crow = sum(_dot(part, tri_t) for part in _split3(grow))

    lane128 = _lane_iota((L, LANES))
    ones_blk = jnp.where(lane128 == 0, 1.0, 0.0).astype(BF16)
    sub128 = _row_iota((LANES, L))

    for h in range(M_HEADS):
        pr, half = divmod(h, 2)
        bc_col = ccol[:, M_HEADS + h:M_HEADS + h + 1]
        bc_row = crow[M_HEADS + h:M_HEADS + h + 1, :]
        ig_row = grow[h:h + 1, :]
        btot = bc_row[:, L - 1:L]
        m0 = m_ref[h][:, 0:1]

        qp = q_ref[0, :, pr * LANES:(pr + 1) * LANES]
        in_half = (lane128 >= half * M_DK) & (lane128 < (half + 1) * M_DK)
        qh = jnp.where(in_half, qp, jnp.zeros_like(qp))
        ktp = kt_ref[0, pr * LANES:(pr + 1) * LANES, :]
        vaug = jnp.concatenate([v_ref[0, :, h * M_DV:(h + 1) * M_DV], ones_blk], axis=1)

        log_d = jnp.where(lower, bc_col - bc_row + ig_row, -jnp.inf)
        m_d = jnp.max(log_d, axis=-1, keepdims=True)
        log_inter = bc_col + m0
        m_t = jnp.maximum(log_inter, m_d)
        d_m = jnp.exp(log_d - m_t)
        inter = jnp.exp(log_inter - m_t)
        sc = (_dot(qh, ktp) * d_m).astype(BF16)
        intra = _dot(sc, vaug)
        cn = cn_ref[h]
        cross = _dot(qh, cn.astype(BF16))
        num = intra[:, 0:M_DV] + inter * cross[:, 0:M_DV]
        den = intra[:, M_DV:M_DV + 1] + inter * cross[:, M_DV:M_DV + 1]
        hh = num / jnp.maximum(jnp.abs(den), jnp.exp(-m_t))
        y = og_ref[0, :, h * M_DV:(h + 1) * M_DV].astype(F32) * hh
        y = _rms(y, g_ref[:, h * M_DV:(h + 1) * M_DV])
        o_ref[0, :, h * M_DV:(h + 1) * M_DV] = y.astype(o_ref.dtype)

        g_row = btot - bc_row + ig_row
        m_loc = jnp.max(g_row, axis=-1, keepdims=True)
        wgt = jnp.exp(g_row - m_loc)
        in_rows = (sub128 >= half * M_DK) & (sub128 < (half + 1) * M_DK)
        ktw = jnp.where(in_rows, ktp.astype(F32) * wgt, 0.0).astype(BF16)
        d_cn = _dot(ktw, vaug)
        m_new = jnp.maximum(btot + m0, m_loc)
        a = jnp.exp(btot + m0 - m_new)
        e = jnp.exp(m_loc - m_new)
        cn_ref[h] = a * cn + e * d_cn
        m_ref[h] = jnp.broadcast_to(m_new, (1, LANES))


def _mlstm(qm, kt, vm, og, gcol, grow, g_out):
    bsz, s, _ = qm.shape
    L = M_CHUNK
    tok = lambda w: pl.BlockSpec((1, L, w), lambda b, c: (b, c, 0))
    return pl.pallas_call(
        _mlstm_kernel,
        out_shape=jax.ShapeDtypeStruct((bsz, s, M_OUT), BF16),
        grid=(bsz, s // L),
        in_specs=[
            tok(M_QK),
            pl.BlockSpec((1, M_QK, L), lambda b, c: (b, 0, c)),
            tok(M_OUT), tok(M_OUT), tok(LANES),
            pl.BlockSpec((1, SUBLANES, L), lambda b, c: (b, 0, c)),
            pl.BlockSpec((1, M_OUT), lambda b, c: (0, 0)),
        ],
        out_specs=tok(M_OUT),
        scratch_shapes=[pltpu.VMEM((M_HEADS, LANES, 2 * M_DV), F32),
                        pltpu.VMEM((M_HEADS, 1, LANES), F32)],
        compiler_params=pltpu.CompilerParams(
            dimension_semantics=("parallel", "arbitrary"), vmem_limit_bytes=VMEM_LIMIT),
        name="mlstm",
    )(qm, kt, vm, og, gcol, grow, g_out)


def _outffn_kernel(x_ref, ya_ref, yb_ref, mod_ref, wout_ref, gffn_ref, wg_ref, wu_ref, wd_ref, gfin_ref,
                   o_ref, *, final_norm):
    x = x_ref[0]
    gate_a = mod_ref[0, 2:3, :]
    shift_f = mod_ref[0, 3:4, :]
    scale_f = mod_ref[0, 4:5, :]
    gate_f = mod_ref[0, 5:6, :]

    mix = _dot(ya_ref[0], wout_ref[0:MLA_OUT, :]) + _dot(yb_ref[0], wout_ref[MLA_OUT:MLA_OUT + M_OUT, :])
    x1 = x + gate_a * mix
    hb = (_rms(x1, gffn_ref[...] * (1.0 + scale_f)) + shift_f).astype(BF16)

    d_ff = wg_ref.shape[1]
    step = -(-d_ff // FF_CHUNKS // (2 * LANES)) * (2 * LANES)
    ffn = None
    for c0 in range(0, d_ff, step):
        c1 = min(c0 + step, d_ff)
        gt = _dot(hb, wg_ref[:, c0:c1])
        up = _dot(hb, wu_ref[:, c0:c1])
        act = (gt * _sigmoid(gt) * up).astype(BF16)
        part = _dot(act, wd_ref[c0:c1, :])
        ffn = part if ffn is None else ffn + part
    x2 = x1 + gate_f * ffn
    if final_norm:
        x2 = _rms(x2, gfin_ref[...])
    o_ref[0] = x2


def _outffn(x, ya, yb, mod, wout, gffn, wg, wu, wd, gfin, final_norm):
    bsz, s, d = x.shape
    tm = TM_OUT
    tok = lambda w: pl.BlockSpec((1, tm, w), lambda b, i: (b, i, 0))
    return pl.pallas_call(
        functools.partial(_outffn_kernel, final_norm=final_norm),
        out_shape=jax.ShapeDtypeStruct((bsz, s, d), F32),
        grid=(bsz, s // tm),
        in_specs=[
            tok(d), tok(MLA_OUT), tok(M_OUT),
            pl.BlockSpec((1, 6, d), lambda b, i: (b, 0, 0)),
            _const_spec(wout.shape),
            _const_spec((1, d)),
            _const_spec(wg.shape), _const_spec(wu.shape), _const_spec(wd.shape),
            _const_spec((1, d)),
        ],
        out_specs=tok(d),
        compiler_params=pltpu.CompilerParams(
            dimension_semantics=("parallel", "parallel"), vmem_limit_bytes=VMEM_LIMIT),
        name="outproj_ffn",
    )(x, ya, yb, mod, wout, gffn, wg, wu, wd, gfin)


def _prep_w_in(w_in):
    d = w_in.shape[0]
    o_q, o_kv, o_kr = 0, MLA_Q_RANK, MLA_Q_RANK + MLA_KV_RANK
    o_qk = o_kr + MLA_ROPE
    o_v = o_qk + 2 * M_QK
    o_o = o_v + M_OUT
    o_i = o_o + M_OUT
    o_f = o_i + M_HEADS
    small = jnp.concatenate([
        w_in[:, o_i:o_i + M_HEADS], w_in[:, o_f:o_f + M_HEADS],
        jnp.zeros((d, ROPE_LO - 2 * M_HEADS), w_in.dtype),
        w_in[:, o_kr:o_kr + MLA_ROPE],
        jnp.zeros((d, HEAD_PAD - ROPE_HI), w_in.dtype)], axis=1)
    return jnp.concatenate([w_in[:, o_q:o_kr], w_in[:, o_qk:o_i], small], axis=1).astype(BF16)


def _prep_w_uq(w_uq):
    r = w_uq.shape[0]
    w = w_uq.reshape(r, MLA_HEADS, MLA_NOPE + MLA_ROPE)
    w = jnp.pad(w, ((0, 0), (0, 0), (0, HEAD_PAD - MLA_NOPE - MLA_ROPE)))
    return w.reshape(r, QK_PAD).astype(BF16)


def _prep_w_ukv(w_ukv):
    r = w_ukv.shape[0]
    w = w_ukv.reshape(r, MLA_HEADS, MLA_NOPE + MLA_V)
    wk = jnp.pad(w[:, :, :MLA_NOPE], ((0, 0), (0, 0), (0, HEAD_PAD - MLA_NOPE))).reshape(r, QK_PAD)
    wv = w[:, :, MLA_NOPE:].reshape(r, MLA_OUT)
    return wk.astype(BF16), wv.astype(BF16)


def kernel(x, c, positions, w_ada, b_ada, g_mix, w_in, g_q, w_uq, g_kv, w_ukv, conv_w, conv_b, b_gates,
           g_out_mla, g_out_mlstm, w_out, g_ffn, w_gate, w_up, w_down, g_final):
    bsz, s, d = x.shape
    depth = w_ada.shape[0]
    cos_blk, sin_blk = _rope_tables(positions)
    row = lambda v: v.reshape(1, -1)
    for l in range(depth):
        mod = _adaln(c, w_ada[l], b_ada[l]).reshape(bsz, 6, d)
        wuk_p, wv_p = _prep_w_ukv(w_ukv[l])
        bg_blk = jnp.pad(b_gates[l], (0, LANES - 2 * M_HEADS)).reshape(1, LANES)
        qa, ka, vt, qm, kt, vm, og, gcol, grow = _inproj(
            x, mod, row(g_mix[l]), _prep_w_in(w_in[l]), row(g_q[l]), _prep_w_uq(w_uq[l]),
            row(g_kv[l]), wuk_p, wv_p, conv_w[l], row(conv_b[l]), bg_blk, cos_blk, sin_blk)
        ya = _mla(qa, ka, vt, row(g_out_mla[l]))
        yb = _mlstm(qm, kt, vm, og, gcol, grow, row(g_out_mlstm[l]))
        x = _outffn(x, ya, yb, mod, w_out[l].astype(BF16), row(g_ffn[l]),
                    w_gate[l].astype(BF16), w_up[l].astype(BF16), w_down[l].astype(BF16),
                    row(g_final), final_norm=(l == depth - 1))
    return x
```

```python
import functools

import numpy as np
import jax
import jax.numpy as jnp
from jax import lax
from jax.experimental import pallas as pl
from jax.experimental.pallas import tpu as pltpu

F32 = jnp.float32
BF16 = jnp.bfloat16

LANES = 128
SUBLANES = 8

MLA_HEADS = 8
MLA_NOPE = 64
MLA_ROPE = 32
MLA_V = 64
MLA_Q_RANK = 384
MLA_KV_RANK = 256
ROPE_THETA = 10000.0
ROPE_HALF = MLA_ROPE // 2
M_HEADS = 4
M_DK = 64
M_DV = 128
CONV_W = 4
EPS = 1e-6
LOG2E = 1.4426950408889634

MLA_OUT = MLA_HEADS * MLA_V
M_OUT = M_HEADS * M_DV
M_QK = M_HEADS * M_DK
HEAD_PAD = LANES
QK_PAD = MLA_HEADS * HEAD_PAD
ROPE_LO = MLA_NOPE
ROPE_MID = MLA_NOPE + ROPE_HALF
ROPE_HI = MLA_NOPE + MLA_ROPE

C_Q = 0
C_KV = C_Q + MLA_Q_RANK
C_QK = C_KV + MLA_KV_RANK
C_V = C_QK + 2 * M_QK
C_O = C_V + M_OUT
C_S = C_O + M_OUT
D_IN_PAD = C_S + LANES

TM_IN = 512
TQ = 512
MLA_HPS = 4
ONES_ROWS = 16
M_CHUNK = 256
TM_OUT = 512
FF_CHUNKS = 3

VMEM_LIMIT = 56 * 1024 * 1024


def _lane_iota(shape):
    return lax.broadcasted_iota(jnp.int32, shape, len(shape) - 1)


def _row_iota(shape):
    return lax.broadcasted_iota(jnp.int32, shape, len(shape) - 2)


def _dot(a, b):
    return jnp.dot(a, b, preferred_element_type=F32)


def _dot_nt(a, b):
    return lax.dot_general(a, b, (((1,), (1,)), ((), ())), preferred_element_type=F32)


def _log_sigmoid(x):
    return jnp.minimum(x, 0.0) - jnp.log1p(jnp.exp(-jnp.abs(x)))


def _sigmoid(x):
    return 1.0 / (1.0 + jnp.exp(-x))


def _adaln_kernel(c_ref, w_ref, b_ref, o_ref):
    c = c_ref[...]
    cond = c * _sigmoid(c)
    o_ref[...] = _dot(cond, w_ref[...]) + b_ref[...]


def _adaln(c, w, b):
    bsz, d = c.shape
    n = w.shape[1]
    tn = 1536
    return pl.pallas_call(
        _adaln_kernel,
        out_shape=jax.ShapeDtypeStruct((bsz, n), F32),
        grid=(n // tn,),
        in_specs=[
            pl.BlockSpec((bsz, d), lambda j: (0, 0)),
            pl.BlockSpec((d, tn), lambda j: (0, j)),
            pl.BlockSpec((1, tn), lambda j: (0, j)),
        ],
        out_specs=pl.BlockSpec((bsz, tn), lambda j: (0, j)),
        compiler_params=pltpu.CompilerParams(
            dimension_semantics=("parallel",), vmem_limit_bytes=VMEM_LIMIT),
        name="adaln",
    )(c, w, b.reshape(1, n))


def _rope_kernel(pos_ref, inv_ref, cos_ref, sin_ref):
    ang = pos_ref[...].astype(F32) * inv_ref[0]
    cos_ref[0] = jnp.cos(ang)
    sin_ref[0] = jnp.sin(ang)


def _rope_tables(positions):
    bsz, s = positions.shape
    rows = bsz * s // LANES
    pos2d = positions.reshape(rows, LANES)
    inv = ROPE_THETA ** (-np.arange(ROPE_HALF, dtype=np.float64) / ROPE_HALF)
    inv = jnp.asarray(np.broadcast_to(inv.astype(np.float32)[:, None, None], (ROPE_HALF, 1, LANES)))
    cos_t, sin_t = pl.pallas_call(
        _rope_kernel,
        out_shape=(jax.ShapeDtypeStruct((ROPE_HALF, rows, LANES), F32),
                   jax.ShapeDtypeStruct((ROPE_HALF, rows, LANES), F32)),
        grid=(ROPE_HALF,),
        in_specs=[
            pl.BlockSpec((rows, LANES), lambda j: (0, 0)),
            pl.BlockSpec((1, 1, LANES), lambda j: (j, 0, 0)),
        ],
        out_specs=(pl.BlockSpec((1, rows, LANES), lambda j: (j, 0, 0)),
                   pl.BlockSpec((1, rows, LANES), lambda j: (j, 0, 0))),
        compiler_params=pltpu.CompilerParams(
            dimension_semantics=("parallel",), vmem_limit_bytes=VMEM_LIMIT),
        name="rope_tables",
    )(pos2d, inv)
    cos = cos_t.reshape(ROPE_HALF, bsz, s).transpose(1, 2, 0)
    sin = sin_t.reshape(ROPE_HALF, bsz, s).transpose(1, 2, 0)
    ones_lo = jnp.ones((bsz, s, ROPE_LO), F32)
    ones_hi = jnp.ones((bsz, s, HEAD_PAD - ROPE_HI), F32)
    cos_blk = jnp.concatenate([ones_lo, cos, cos, ones_hi], axis=-1)
    sin_blk = jnp.concatenate([0.0 * ones_lo, -sin, sin, 0.0 * ones_hi], axis=-1)
    return cos_blk, sin_blk


def _rms(x, g):
    return x * lax.rsqrt(jnp.mean(x * x, axis=-1, keepdims=True) + EPS) * g


def _rope_block(x, cos_blk, sin_blk, lane):
    fwd = pltpu.roll(x, HEAD_PAD - ROPE_HALF, axis=1)
    bwd = pltpu.roll(x, ROPE_HALF, axis=1)
    rot = jnp.where(lane < ROPE_MID, fwd, bwd)
    return x * cos_blk + rot * sin_blk


def _inproj_kernel(x_ref, mod_ref, gmix_ref, win_ref, gq_ref, wuq_ref, gkv_ref, wuk_ref, wv_ref,
                   convw_ref, convb_ref, bg_ref, cos_ref, sin_ref,
                   qa_ref, ka_ref, vt_ref, qmt_ref, km_ref, vmt_ref, og_ref, gcol_ref, grow_ref,
                   zbuf_ref):
    tm = x_ref.shape[1]
    si = pl.program_id(1)

    x = x_ref[0]
    shift = mod_ref[0, 0:1, :]
    scale = mod_ref[0, 1:2, :]
    h = _rms(x, gmix_ref[...] * (1.0 + scale)) + shift
    hb = h.astype(BF16)

    cos_blk = cos_ref[0]
    sin_blk = sin_ref[0]
    lane = _lane_iota((tm, LANES))

    ql = _dot(hb, win_ref[:, C_Q:C_KV])
    qn = _rms(ql, gq_ref[...]).astype(BF16)
    qa = _dot(qn, wuq_ref[...]) * ((MLA_NOPE + MLA_ROPE) ** -0.5 * LOG2E)
    for hd in range(MLA_HEADS):
        blk = qa[:, hd * HEAD_PAD:(hd + 1) * HEAD_PAD]
        qa_ref[0, :, hd * HEAD_PAD:(hd + 1) * HEAD_PAD] = _rope_block(blk, cos_blk, sin_blk, lane).astype(BF16)

    zs = _dot(hb, win_ref[:, C_S:D_IN_PAD])
    kr = _rope_block(zs, cos_blk, sin_blk, lane)
    kr = jnp.where((lane >= ROPE_LO) & (lane < ROPE_HI), kr, 0.0)
    gpre = zs + bg_ref[...]
    gates = jnp.where(lane < M_HEADS, gpre, _log_sigmoid(gpre))
    gcol_ref[0] = gates
    grow_ref[0] = jnp.transpose(gates)[0:SUBLANES, :]

    kl = _dot(hb, win_ref[:, C_KV:C_QK])
    kn = _rms(kl, gkv_ref[...]).astype(BF16)
    ka = _dot(kn, wuk_ref[...])
    for hd in range(MLA_HEADS):
        ka_ref[0, :, hd * HEAD_PAD:(hd + 1) * HEAD_PAD] = (ka[:, hd * HEAD_PAD:(hd + 1) * HEAD_PAD] + kr).astype(BF16)
    vt_ref[0] = jnp.transpose(_dot(kn, wv_ref[...])).astype(BF16)

    @pl.when(si == 0)
    def _():
        zbuf_ref[0:SUBLANES, :] = jnp.zeros((SUBLANES, 2 * M_QK), F32)

    zqk = _dot(hb, win_ref[:, C_QK:C_V])
    zbuf_ref[SUBLANES:SUBLANES + tm, :] = zqk
    acc = zqk * convw_ref[CONV_W - 1:CONV_W, :] + convb_ref[...]
    for j in range(1, CONV_W):
        acc = acc + zbuf_ref[SUBLANES - j:SUBLANES - j + tm, :] * convw_ref[CONV_W - 1 - j:CONV_W - j, :]
    zbuf_ref[0:SUBLANES, :] = zqk[tm - SUBLANES:tm, :]
    qk = acc * _sigmoid(acc)
    qmt_ref[0] = jnp.transpose(qk[:, 0:M_QK]).astype(BF16)
    km_ref[0] = (qk[:, M_QK:2 * M_QK] * (M_DK ** -0.5)).astype(BF16)

    vmt_ref[0] = jnp.transpose(_dot(hb, win_ref[:, C_V:C_O])).astype(BF16)
    og_ref[0] = _sigmoid(_dot(hb, win_ref[:, C_O:C_S])).astype(BF16)


def _const_spec(shape):
    nd = len(shape)
    return pl.BlockSpec(shape, lambda *_: (0,) * nd, pipeline_mode=pl.Buffered(1))


def _inproj(x, mod, gmix, win_p, gq, wuq_p, gkv, wuk_p, wv_p, convw, convb, bg_blk, cos_blk, sin_blk):
    bsz, s, d = x.shape
    tm = TM_IN
    tok = lambda w: pl.BlockSpec((1, tm, w), lambda b, i: (b, i, 0))
    out_shapes = (
        jax.ShapeDtypeStruct((bsz, s, QK_PAD), BF16),
        jax.ShapeDtypeStruct((bsz, s, QK_PAD), BF16),
        jax.ShapeDtypeStruct((bsz, MLA_OUT, s), BF16),
        jax.ShapeDtypeStruct((bsz, M_QK, s), BF16),
        jax.ShapeDtypeStruct((bsz, s, M_QK), BF16),
        jax.ShapeDtypeStruct((bsz, M_OUT, s), BF16),
        jax.ShapeDtypeStruct((bsz, s, M_OUT), BF16),
        jax.ShapeDtypeStruct((bsz, s, LANES), F32),
        jax.ShapeDtypeStruct((bsz, SUBLANES, s), F32),
    )
    out_specs = (
        tok(QK_PAD), tok(QK_PAD),
        pl.BlockSpec((1, MLA_OUT, tm), lambda b, i: (b, 0, i)),
        pl.BlockSpec((1, M_QK, tm), lambda b, i: (b, 0, i)),
        tok(M_QK),
        pl.BlockSpec((1, M_OUT, tm), lambda b, i: (b, 0, i)),
        tok(M_OUT), tok(LANES),
        pl.BlockSpec((1, SUBLANES, tm), lambda b, i: (b, 0, i)),
    )
    in_specs = [
        tok(d),
        pl.BlockSpec((1, 6, d), lambda b, i: (b, 0, 0)),
        _const_spec((1, d)),
        _const_spec(win_p.shape),
        _const_spec((1, MLA_Q_RANK)),
        _const_spec(wuq_p.shape),
        _const_spec((1, MLA_KV_RANK)),
        _const_spec(wuk_p.shape),
        _const_spec(wv_p.shape),
        _const_spec((CONV_W, 2 * M_QK)),
        _const_spec((1, 2 * M_QK)),
        _const_spec((1, LANES)),
        tok(LANES), tok(LANES),
    ]
    return pl.pallas_call(
        _inproj_kernel,
        out_shape=out_shapes,
        grid=(bsz, s // tm),
        in_specs=in_specs,
        out_specs=out_specs,
        scratch_shapes=[pltpu.VMEM((tm + SUBLANES, 2 * M_QK), F32)],
        compiler_params=pltpu.CompilerParams(
            dimension_semantics=("parallel", "arbitrary"), vmem_limit_bytes=VMEM_LIMIT),
        name="inproj",
    )(x, mod, gmix, win_p, gq, wuq_p, gkv, wuk_p, wv_p, convw, convb, bg_blk, cos_blk, sin_blk)


def _mla_kernel(q_ref, k_ref, vt_ref, g_ref, o_ref):
    s = q_ref.shape[1]
    nq = s // TQ
    causal_t = _row_iota((TQ, TQ)) <= _lane_iota((TQ, TQ))
    ones_rows = jnp.ones((ONES_ROWS, TQ), BF16)

    def q_tile(qi, carry):
        q0 = pl.multiple_of(qi * TQ, TQ)
        qs = [q_ref[0, pl.ds(q0, TQ), h * HEAD_PAD:(h + 1) * HEAD_PAD] for h in range(MLA_HPS)]

        def scores(k0):
            return tuple(_dot_nt(k_ref[0, pl.ds(k0, TQ), h * HEAD_PAD:(h + 1) * HEAD_PAD], qs[h])
                         for h in range(MLA_HPS))

        def kv_step(k0, sts, state, masked):
            soft, pvs = [], []
            for h in range(MLA_HPS):
                m = state[h][0]
                st = jnp.where(causal_t, sts[h], -jnp.inf) if masked else sts[h]
                m_new = jnp.maximum(m, jnp.max(st, axis=0, keepdims=True))
                soft.append((m_new, jnp.exp2(m - m_new)))
                p = jnp.exp2(st - m_new).astype(BF16)
                vt = jnp.concatenate([vt_ref[0, h * MLA_V:(h + 1) * MLA_V, pl.ds(k0, TQ)], ones_rows], axis=0)
                pvs.append(_dot(vt, p))
            return tuple((soft[h][0], soft[h][1] * state[h][1] + pvs[h]) for h in range(MLA_HPS))

        def body(kj, state):
            k0 = pl.multiple_of(kj * TQ, TQ)
            return kv_step(k0, scores(k0), state, False)

        init = tuple((jnp.full((1, TQ), -jnp.inf, F32), jnp.zeros((MLA_V + ONES_ROWS, TQ), F32))
                     for _ in range(MLA_HPS))
        state = lax.fori_loop(0, qi, body, init)
        state = kv_step(q0, scores(q0), state, True)

        for pr in range(MLA_HPS // 2):
            halves = []
            for h in (2 * pr, 2 * pr + 1):
                acc = state[h][1]
                o = acc[0:MLA_V] / acc[MLA_V:MLA_V + 1]
                ms = jnp.mean(o * o, axis=0, keepdims=True)
                halves.append(o * lax.rsqrt(ms + EPS))
            y = jnp.transpose(jnp.concatenate(halves, axis=0))
            y = y * g_ref[:, pr * LANES:(pr + 1) * LANES]
            o_ref[0, pl.ds(q0, TQ), pr * LANES:(pr + 1) * LANES] = y.astype(o_ref.dtype)
        return carry

    lax.fori_loop(0, nq, q_tile, 0)


def _mla(qa, ka, vt, g_out):
    bsz, s, _ = qa.shape
    groups = MLA_HEADS // MLA_HPS
    return pl.pallas_call(
        _mla_kernel,
        out_shape=jax.ShapeDtypeStruct((bsz, s, MLA_OUT), BF16),
        grid=(bsz, groups),
        in_specs=[
            pl.BlockSpec((1, s, MLA_HPS * HEAD_PAD), lambda b, p: (b, 0, p)),
            pl.BlockSpec((1, s, MLA_HPS * HEAD_PAD), lambda b, p: (b, 0, p)),
            pl.BlockSpec((1, MLA_HPS * MLA_V, s), lambda b, p: (b, p, 0)),
            pl.BlockSpec((1, MLA_HPS * MLA_V), lambda b, p: (0, p)),
        ],
        out_specs=pl.BlockSpec((1, s, MLA_HPS * MLA_V), lambda b, p: (b, 0, p)),
        compiler_params=pltpu.CompilerParams(
            dimension_semantics=("parallel", "parallel"), vmem_limit_bytes=VMEM_LIMIT),
        name="mla_attention",
    )(qa, ka, vt, g_out)


def _split3(x):
    hi = x.astype(BF16)
    r1 = x - hi.astype(F32)
    mid = r1.astype(BF16)
    lo = (r1 - mid.astype(F32)).astype(BF16)
    return hi, mid, lo


def _mlstm_kernel(qt_ref, k_ref, vt_ref, og_ref, gcol_ref, grow_ref, g_ref, o_ref, cn_ref, m_ref):
    L = k_ref.shape[1]
    ci = pl.program_id(1)

    @pl.when(ci == 0)
    def _():
        cn_ref[...] = jnp.zeros(cn_ref.shape, F32)
        m_ref[...] = jnp.zeros(m_ref.shape, F32)

    row = _row_iota((L, L))
    col = _lane_iota((L, L))
    upper = row <= col
    tri = jnp.where(row >= col, 1.0, 0.0).astype(BF16)
    tri_t = jnp.where(upper, 1.0, 0.0).astype(BF16)

    gcol = gcol_ref[0]
    grow = grow_ref[0]
    ccol = sum(_dot(tri, part) for part in _split3(gcol))
    crow = sum(_dot(part, tri_t) for part in _split3(grow))

    lane1 = _lane_iota((1, LANES))
    head_lane = (lane1 >= M_HEADS) & (lane1 < 2 * M_HEADS)
    c_all = ccol - pltpu.roll(gcol, M_HEADS, axis=1)
    btot_all = ccol[L - 1:L, :]
    m0_all = m_ref[...]
    g_all = btot_all - c_all
    m_loc_all = jnp.max(g_all, axis=0, keepdims=True)
    wgt_all = jnp.exp(g_all - m_loc_all)
    m_new_all = jnp.where(head_lane, jnp.maximum(btot_all + m0_all, m_loc_all), 0.0)
    a_all = jnp.exp(btot_all + m0_all - m_new_all)
    e_all = jnp.exp(m_loc_all - m_new_all)
    m_ref[...] = m_new_all

    lane128 = _lane_iota((L, LANES))
    ones_rows = jnp.ones((ONES_ROWS, L), BF16)

    def head_keys(h):
        pr, half = divmod(h, 2)
        kp = k_ref[0, :, pr * LANES:(pr + 1) * LANES]
        in_half = (lane128 >= half * M_DK) & (lane128 < (half + 1) * M_DK)
        return kp, in_half

    sts = []
    for h in range(M_HEADS):
        kp, in_half = head_keys(h)
        kh = jnp.where(in_half, kp, jnp.zeros_like(kp))
        sts.append(_dot(kh, qt_ref[0, (h // 2) * LANES:(h // 2 + 1) * LANES, :]))

    for h in range(M_HEADS):
        hl = M_HEADS + h
        bc_row = crow[hl:hl + 1, :]
        m0 = m0_all[:, hl:hl + 1]
        log_d = jnp.where(upper, bc_row - c_all[:, hl:hl + 1], -jnp.inf)
        m_d = jnp.max(log_d, axis=0, keepdims=True)
        log_inter = bc_row + m0
        m_t = jnp.maximum(log_inter, m_d)
        d_m = jnp.exp(log_d - m_t)
        inter = jnp.exp(log_inter - m_t)
        sc = (sts[h] * d_m).astype(BF16)
        vaug = jnp.concatenate([vt_ref[0, h * M_DV:(h + 1) * M_DV, :], ones_rows], axis=0)
        intra = _dot(vaug, sc)
        cn = cn_ref[h]
        cross = _dot(cn.astype(BF16), qt_ref[0, (h // 2) * LANES:(h // 2 + 1) * LANES, :])
        num = intra[0:M_DV] + inter * cross[0:M_DV]
        den = intra[M_DV:M_DV + 1] + inter * cross[M_DV:M_DV + 1]
        hh = jnp.transpose(num / jnp.maximum(jnp.abs(den), jnp.exp(-m_t)))
        y = og_ref[0, :, h * M_DV:(h + 1) * M_DV].astype(F32) * hh
        y = _rms(y, g_ref[:, h * M_DV:(h + 1) * M_DV])
        o_ref[0, :, h * M_DV:(h + 1) * M_DV] = y.astype(o_ref.dtype)

        kp, in_half = head_keys(h)
        kw = (jnp.where(in_half, kp.astype(F32), 0.0) * wgt_all[:, hl:hl + 1]).astype(BF16)
        d_cn = _dot(vaug, kw)
        cn_ref[h] = a_all[:, hl:hl + 1] * cn + e_all[:, hl:hl + 1] * d_cn


def _mlstm(qmt, km, vmt, og, gcol, grow, g_out):
    bsz, s, _ = km.shape
    L = M_CHUNK
    tok = lambda w: pl.BlockSpec((1, L, w), lambda b, c: (b, c, 0))
    tok_t = lambda w: pl.BlockSpec((1, w, L), lambda b, c: (b, 0, c))
    return pl.pallas_call(
        _mlstm_kernel,
        out_shape=jax.ShapeDtypeStruct((bsz, s, M_OUT), BF16),
        grid=(bsz, s // L),
        in_specs=[
            tok_t(M_QK), tok(M_QK), tok_t(M_OUT), tok(M_OUT), tok(LANES), tok_t(SUBLANES),
            pl.BlockSpec((1, M_OUT), lambda b, c: (0, 0)),
        ],
        out_specs=tok(M_OUT),
        scratch_shapes=[pltpu.VMEM((M_HEADS, M_DV + ONES_ROWS, LANES), F32),
                        pltpu.VMEM((1, LANES), F32)],
        compiler_params=pltpu.CompilerParams(
            dimension_semantics=("parallel", "arbitrary"), vmem_limit_bytes=VMEM_LIMIT),
        name="mlstm",
    )(qmt, km, vmt, og, gcol, grow, g_out)


def _outffn_kernel(x_ref, ya_ref, yb_ref, mod_ref, wout_ref, gffn_ref, wg_ref, wu_ref, wd_ref, gfin_ref,
                   o_ref, *, final_norm):
    x = x_ref[0]
    gate_a = mod_ref[0, 2:3, :]
    shift_f = mod_ref[0, 3:4, :]
    scale_f = mod_ref[0, 4:5, :]
    gate_f = mod_ref[0, 5:6, :]

    mix = _dot(ya_ref[0], wout_ref[0:MLA_OUT, :]) + _dot(yb_ref[0], wout_ref[MLA_OUT:MLA_OUT + M_OUT, :])
    x1 = x + gate_a * mix
    hb = (_rms(x1, gffn_ref[...] * (1.0 + scale_f)) + shift_f).astype(BF16)

    d_ff = wg_ref.shape[1]
    step = -(-d_ff // FF_CHUNKS // (2 * LANES)) * (2 * LANES)
    ffn = None
    for c0 in range(0, d_ff, step):
        c1 = min(c0 + step, d_ff)
        gt = _dot(hb, wg_ref[:, c0:c1])
        up = _dot(hb, wu_ref[:, c0:c1])
        act = (gt * _sigmoid(gt) * up).astype(BF16)
        part = _dot(act, wd_ref[c0:c1, :])
        ffn = part if ffn is None else ffn + part
    x2 = x1 + gate_f * ffn
    if final_norm:
        x2 = _rms(x2, gfin_ref[...])
    o_ref[0] = x2


def _outffn(x, ya, yb, mod, wout, gffn, wg, wu, wd, gfin, final_norm):
    bsz, s, d = x.shape
    tm = TM_OUT
    tok = lambda w: pl.BlockSpec((1, tm, w), lambda b, i: (b, i, 0))
    return pl.pallas_call(
        functools.partial(_outffn_kernel, final_norm=final_norm),
        out_shape=jax.ShapeDtypeStruct((bsz, s, d), F32),
        grid=(bsz, s // tm),
        in_specs=[
            tok(d), tok(MLA_OUT), tok(M_OUT),
            pl.BlockSpec((1, 6, d), lambda b, i: (b, 0, 0)),
            _const_spec(wout.shape),
            _const_spec((1, d)),
            _const_spec(wg.shape), _const_spec(wu.shape), _const_spec(wd.shape),
            _const_spec((1, d)),
        ],
        out_specs=tok(d),
        compiler_params=pltpu.CompilerParams(
            dimension_semantics=("parallel", "parallel"), vmem_limit_bytes=VMEM_LIMIT),
        name="outproj_ffn",
    )(x, ya, yb, mod, wout, gffn, wg, wu, wd, gfin)


def _prep_w_in(w_in):
    d = w_in.shape[0]
    o_q, o_kv, o_kr = 0, MLA_Q_RANK, MLA_Q_RANK + MLA_KV_RANK
    o_qk = o_kr + MLA_ROPE
    o_v = o_qk + 2 * M_QK
    o_o = o_v + M_OUT
    o_i = o_o + M_OUT
    o_f = o_i + M_HEADS
    small = jnp.concatenate([
        w_in[:, o_i:o_i + M_HEADS], w_in[:, o_f:o_f + M_HEADS],
        jnp.zeros((d, ROPE_LO - 2 * M_HEADS), w_in.dtype),
        w_in[:, o_kr:o_kr + MLA_ROPE],
        jnp.zeros((d, HEAD_PAD - ROPE_HI), w_in.dtype)], axis=1)
    return jnp.concatenate([w_in[:, o_q:o_kr], w_in[:, o_qk:o_i], small], axis=1).astype(BF16)


def _prep_w_uq(w_uq):
    r = w_uq.shape[0]
    w = w_uq.reshape(r, MLA_HEADS, MLA_NOPE + MLA_ROPE)
    w = jnp.pad(w, ((0, 0), (0, 0), (0, HEAD_PAD - MLA_NOPE - MLA_ROPE)))
    return w.reshape(r, QK_PAD).astype(BF16)


def _prep_w_ukv(w_ukv):
    r = w_ukv.shape[0]
    w = w_ukv.reshape(r, MLA_HEADS, MLA_NOPE + MLA_V)
    wk = jnp.pad(w[:, :, :MLA_NOPE], ((0, 0), (0, 0), (0, HEAD_PAD - MLA_NOPE))).reshape(r, QK_PAD)
    wv = w[:, :, MLA_NOPE:].reshape(r, MLA_OUT)
    return wk.astype(BF16), wv.astype(BF16)


def kernel(x, c, positions, w_ada, b_ada, g_mix, w_in, g_q, w_uq, g_kv, w_ukv, conv_w, conv_b, b_gates,
           g_out_mla, g_out_mlstm, w_out, g_ffn, w_gate, w_up, w_down, g_final):
    bsz, s, d = x.shape
    depth = w_ada.shape[0]
    cos_blk, sin_blk = _rope_tables(positions)
    row = lambda v: v.reshape(1, -1)
    for l in range(depth):
        mod = _adaln(c, w_ada[l], b_ada[l]).reshape(bsz, 6, d)
        wuk_p, wv_p = _prep_w_ukv(w_ukv[l])
        bg_blk = jnp.pad(b_gates[l], (0, LANES - 2 * M_HEADS)).reshape(1, LANES)
        qa, ka, vt, qmt, km, vmt, og, gcol, grow = _inproj(
            x, mod, row(g_mix[l]), _prep_w_in(w_in[l]), row(g_q[l]), _prep_w_uq(w_uq[l]),
            row(g_kv[l]), wuk_p, wv_p, conv_w[l], row(conv_b[l]), bg_blk, cos_blk, sin_blk)
        ya = _mla(qa, ka, vt, row(g_out_mla[l]))
        yb = _mlstm(qmt, km, vmt, og, gcol, grow, row(g_out_mlstm[l]))
        x = _outffn(x, ya, yb, mod, w_out[l].astype(BF16), row(g_ffn[l]),
                    w_gate[l].astype(BF16), w_up[l].astype(BF16), w_down[l].astype(BF16),
                    row(g_final), final_norm=(l == depth - 1))
    return x
```

```python
import functools

import numpy as np
import jax
import jax.numpy as jnp
from jax import lax
from jax.experimental import pallas as pl
from jax.experimental.pallas import tpu as pltpu

F32 = jnp.float32
BF16 = jnp.bfloat16

LANES = 128
SUBLANES = 8

MLA_HEADS = 8
MLA_NOPE = 64
MLA_ROPE = 32
MLA_V = 64
MLA_Q_RANK = 384
MLA_KV_RANK = 256
ROPE_THETA = 10000.0
ROPE_HALF = MLA_ROPE // 2
M_HEADS = 4
M_DK = 64
M_DV = 128
CONV_W = 4
EPS = 1e-6
LOG2E = 1.4426950408889634

MLA_OUT = MLA_HEADS * MLA_V
M_OUT = M_HEADS * M_DV
M_QK = M_HEADS * M_DK
HEAD_PAD = LANES
QK_PAD = MLA_HEADS * HEAD_PAD
ROPE_LO = MLA_NOPE
ROPE_MID = MLA_NOPE + ROPE_HALF
ROPE_HI = MLA_NOPE + MLA_ROPE

C_Q = 0
C_KV = C_Q + MLA_Q_RANK
C_QK = C_KV + MLA_KV_RANK
C_V = C_QK + 2 * M_QK
C_O = C_V + M_OUT
C_S = C_O + M_OUT
D_IN_PAD = C_S + LANES

TM_IN = 1024
TM_SUB = 512
TQ = 512
MLA_HPS = 4
ONES_ROWS = 16
M_CHUNK = 256
TM_OUT = 512
FF_CHUNKS = 3

VMEM_LIMIT = 56 * 1024 * 1024


def _lane_iota(shape):
    return lax.broadcasted_iota(jnp.int32, shape, len(shape) - 1)


def _row_iota(shape):
    return lax.broadcasted_iota(jnp.int32, shape, len(shape) - 2)


def _dot(a, b):
    return jnp.dot(a, b, preferred_element_type=F32)


def _dot_nt(a, b):
    return lax.dot_general(a, b, (((1,), (1,)), ((), ())), preferred_element_type=F32)


def _log_sigmoid(x):
    return jnp.minimum(x, 0.0) - jnp.log1p(jnp.exp(-jnp.abs(x)))


def _sigmoid(x):
    return 1.0 / (1.0 + jnp.exp(-x))


def _adaln_kernel(c_ref, w_ref, b_ref, o_ref):
    c = c_ref[...]
    cond = c * _sigmoid(c)
    o_ref[...] = _dot(cond, w_ref[...]) + b_ref[...]


def _adaln(c, w, b):
    bsz, d = c.shape
    n = w.shape[1]
    tn = 1536
    return pl.pallas_call(
        _adaln_kernel,
        out_shape=jax.ShapeDtypeStruct((bsz, n), F32),
        grid=(n // tn,),
        in_specs=[
            pl.BlockSpec((bsz, d), lambda j: (0, 0)),
            pl.BlockSpec((d, tn), lambda j: (0, j)),
            pl.BlockSpec((1, tn), lambda j: (0, j)),
        ],
        out_specs=pl.BlockSpec((bsz, tn), lambda j: (0, j)),
        compiler_params=pltpu.CompilerParams(
            dimension_semantics=("parallel",), vmem_limit_bytes=VMEM_LIMIT),
        name="adaln",
    )(c, w, b.reshape(1, n))


def _rope_kernel(pos_ref, inv_ref, cos_ref, sin_ref):
    ang = pos_ref[...].astype(F32) * inv_ref[0]
    cos_ref[0] = jnp.cos(ang)
    sin_ref[0] = jnp.sin(ang)


def _rope_tables(positions):
    bsz, s = positions.shape
    rows = bsz * s // LANES
    pos2d = positions.reshape(rows, LANES)
    inv = ROPE_THETA ** (-np.arange(ROPE_HALF, dtype=np.float64) / ROPE_HALF)
    inv = jnp.asarray(np.broadcast_to(inv.astype(np.float32)[:, None, None], (ROPE_HALF, 1, LANES)))
    cos_t, sin_t = pl.pallas_call(
        _rope_kernel,
        out_shape=(jax.ShapeDtypeStruct((ROPE_HALF, rows, LANES), F32),
                   jax.ShapeDtypeStruct((ROPE_HALF, rows, LANES), F32)),
        grid=(ROPE_HALF,),
        in_specs=[
            pl.BlockSpec((rows, LANES), lambda j: (0, 0)),
            pl.BlockSpec((1, 1, LANES), lambda j: (j, 0, 0)),
        ],
        out_specs=(pl.BlockSpec((1, rows, LANES), lambda j: (j, 0, 0)),
                   pl.BlockSpec((1, rows, LANES), lambda j: (j, 0, 0))),
        compiler_params=pltpu.CompilerParams(
            dimension_semantics=("parallel",), vmem_limit_bytes=VMEM_LIMIT),
        name="rope_tables",
    )(pos2d, inv)
    cos = cos_t.reshape(ROPE_HALF, bsz, s).transpose(1, 2, 0)
    sin = sin_t.reshape(ROPE_HALF, bsz, s).transpose(1, 2, 0)
    ones_lo = jnp.ones((bsz, s, ROPE_LO), F32)
    ones_hi = jnp.ones((bsz, s, HEAD_PAD - ROPE_HI), F32)
    cos_blk = jnp.concatenate([ones_lo, cos, cos, ones_hi], axis=-1)
    sin_blk = jnp.concatenate([0.0 * ones_lo, -sin, sin, 0.0 * ones_hi], axis=-1)
    return cos_blk, sin_blk


def _rms(x, g):
    return x * lax.rsqrt(jnp.mean(x * x, axis=-1, keepdims=True) + EPS) * g


def _rope_block(x, cos_blk, sin_blk, lane):
    fwd = pltpu.roll(x, HEAD_PAD - ROPE_HALF, axis=1)
    bwd = pltpu.roll(x, ROPE_HALF, axis=1)
    rot = jnp.where(lane < ROPE_MID, fwd, bwd)
    return x * cos_blk + rot * sin_blk


def _inproj_kernel(x_ref, mod_ref, gmix_ref, win_ref, gq_ref, wuq_ref, gkv_ref, wuk_ref, wv_ref,
                   convw_ref, convb_ref, bg_ref, cos_ref, sin_ref,
                   qa_ref, ka_ref, vt_ref, qmt_ref, km_ref, vmt_ref, og_ref, gcol_ref, grow_ref,
                   zbuf_ref):
    tm = x_ref.shape[1]
    si = pl.program_id(1)
    shift = mod_ref[0, 0:1, :]
    gain = gmix_ref[...] * (1.0 + mod_ref[0, 1:2, :])
    lane = _lane_iota((TM_SUB, LANES))

    @pl.when(si == 0)
    def _():
        zbuf_ref[0:SUBLANES, :] = jnp.zeros((SUBLANES, 2 * M_QK), F32)

    for r0 in range(0, tm, TM_SUB):
        rows = slice(r0, r0 + TM_SUB)
        hb = (_rms(x_ref[0, rows, :], gain) + shift).astype(BF16)
        cos_blk = cos_ref[0, rows, :]
        sin_blk = sin_ref[0, rows, :]

        ql = _dot(hb, win_ref[:, C_Q:C_KV])
        qn = _rms(ql, gq_ref[...]).astype(BF16)
        qa = _dot(qn, wuq_ref[...]) * ((MLA_NOPE + MLA_ROPE) ** -0.5 * LOG2E)
        for hd in range(MLA_HEADS):
            blk = qa[:, hd * HEAD_PAD:(hd + 1) * HEAD_PAD]
            qa_ref[0, rows, hd * HEAD_PAD:(hd + 1) * HEAD_PAD] = (
                _rope_block(blk, cos_blk, sin_blk, lane).astype(BF16))

        zs = _dot(hb, win_ref[:, C_S:D_IN_PAD])
        kr = _rope_block(zs, cos_blk, sin_blk, lane)
        kr = jnp.where((lane >= ROPE_LO) & (lane < ROPE_HI), kr, 0.0)
        gpre = zs + bg_ref[...]
        gates = jnp.where(lane < M_HEADS, gpre, _log_sigmoid(gpre))
        gcol_ref[0, rows, :] = gates
        grow_ref[0, :, rows] = jnp.transpose(gates)[0:SUBLANES, :]

        kl = _dot(hb, win_ref[:, C_KV:C_QK])
        kn = _rms(kl, gkv_ref[...]).astype(BF16)
        ka = _dot(kn, wuk_ref[...])
        for hd in range(MLA_HEADS):
            ka_ref[0, rows, hd * HEAD_PAD:(hd + 1) * HEAD_PAD] = (
                ka[:, hd * HEAD_PAD:(hd + 1) * HEAD_PAD] + kr).astype(BF16)
        vt_ref[0, :, rows] = jnp.transpose(_dot(kn, wv_ref[...])).astype(BF16)

        zqk = _dot(hb, win_ref[:, C_QK:C_V])
        z0 = SUBLANES + r0
        zbuf_ref[z0:z0 + TM_SUB, :] = zqk
        acc = zqk * convw_ref[CONV_W - 1:CONV_W, :] + convb_ref[...]
        for j in range(1, CONV_W):
            acc = acc + zbuf_ref[z0 - j:z0 - j + TM_SUB, :] * convw_ref[CONV_W - 1 - j:CONV_W - j, :]
        qk = acc * _sigmoid(acc)
        qmt_ref[0, :, rows] = jnp.transpose(qk[:, 0:M_QK]).astype(BF16)
        km_ref[0, rows, :] = (qk[:, M_QK:2 * M_QK] * (M_DK ** -0.5)).astype(BF16)

        vmt_ref[0, :, rows] = jnp.transpose(_dot(hb, win_ref[:, C_V:C_O])).astype(BF16)
        og_ref[0, rows, :] = _sigmoid(_dot(hb, win_ref[:, C_O:C_S])).astype(BF16)

    zbuf_ref[0:SUBLANES, :] = zbuf_ref[tm:tm + SUBLANES, :]


def _const_spec(shape):
    nd = len(shape)
    return pl.BlockSpec(shape, lambda *_: (0,) * nd, pipeline_mode=pl.Buffered(1))


def _inproj(x, mod, gmix, win_p, gq, wuq_p, gkv, wuk_p, wv_p, convw, convb, bg_blk, cos_blk, sin_blk):
    bsz, s, d = x.shape
    tm = TM_IN
    tok = lambda w: pl.BlockSpec((1, tm, w), lambda b, i: (b, i, 0))
    out_shapes = (
        jax.ShapeDtypeStruct((bsz, s, QK_PAD), BF16),
        jax.ShapeDtypeStruct((bsz, s, QK_PAD), BF16),
        jax.ShapeDtypeStruct((bsz, MLA_OUT, s), BF16),
        jax.ShapeDtypeStruct((bsz, M_QK, s), BF16),
        jax.ShapeDtypeStruct((bsz, s, M_QK), BF16),
        jax.ShapeDtypeStruct((bsz, M_OUT, s), BF16),
        jax.ShapeDtypeStruct((bsz, s, M_OUT), BF16),
        jax.ShapeDtypeStruct((bsz, s, LANES), F32),
        jax.ShapeDtypeStruct((bsz, SUBLANES, s), F32),
    )
    out_specs = (
        tok(QK_PAD), tok(QK_PAD),
        pl.BlockSpec((1, MLA_OUT, tm), lambda b, i: (b, 0, i)),
        pl.BlockSpec((1, M_QK, tm), lambda b, i: (b, 0, i)),
        tok(M_QK),
        pl.BlockSpec((1, M_OUT, tm), lambda b, i: (b, 0, i)),
        tok(M_OUT), tok(LANES),
        pl.BlockSpec((1, SUBLANES, tm), lambda b, i: (b, 0, i)),
    )
    in_specs = [
        tok(d),
        pl.BlockSpec((1, 6, d), lambda b, i: (b, 0, 0)),
        _const_spec((1, d)),
        _const_spec(win_p.shape),
        _const_spec((1, MLA_Q_RANK)),
        _const_spec(wuq_p.shape),
        _const_spec((1, MLA_KV_RANK)),
        _const_spec(wuk_p.shape),
        _const_spec(wv_p.shape),
        _const_spec((CONV_W, 2 * M_QK)),
        _const_spec((1, 2 * M_QK)),
        _const_spec((1, LANES)),
        tok(LANES), tok(LANES),
    ]
    return pl.pallas_call(
        _inproj_kernel,
        out_shape=out_shapes,
        grid=(bsz, s // tm),
        in_specs=in_specs,
        out_specs=out_specs,
        scratch_shapes=[pltpu.VMEM((tm + SUBLANES, 2 * M_QK), F32)],
        compiler_params=pltpu.CompilerParams(
            dimension_semantics=("parallel", "arbitrary"), vmem_limit_bytes=VMEM_LIMIT),
        name="inproj",
    )(x, mod, gmix, win_p, gq, wuq_p, gkv, wuk_p, wv_p, convw, convb, bg_blk, cos_blk, sin_blk)


def _mla_kernel(q_ref, k_ref, vt_ref, g_ref, o_ref, m_ref, acc_ref):
    s = q_ref.shape[1]
    causal_t = _row_iota((TQ, TQ)) <= _lane_iota((TQ, TQ))
    ones_rows = jnp.ones((ONES_ROWS, TQ), BF16)

    def q_pair(a, carry):
        q0s = [pl.multiple_of((2 * a + t) * TQ, TQ) for t in range(2)]
        qs = [[q_ref[0, pl.ds(q0s[t], TQ), h * HEAD_PAD:(h + 1) * HEAD_PAD] for h in range(MLA_HPS)]
              for t in range(2)]
        for c in range(2 * MLA_HPS):
            m_ref[c] = jnp.full((1, TQ), -jnp.inf, F32)
            acc_ref[c] = jnp.zeros((MLA_V + ONES_ROWS, TQ), F32)

        def kv_step(k0, tiles, masked_tiles):
            chains = [(t, h) for t in tiles for h in range(MLA_HPS)]
            ks = [k_ref[0, pl.ds(k0, TQ), h * HEAD_PAD:(h + 1) * HEAD_PAD] for h in range(MLA_HPS)]
            sts = [_dot_nt(ks[h], qs[t][h]) for t, h in chains]
            upd = []
            for (t, h), st in zip(chains, sts):
                c = t * MLA_HPS + h
                if t in masked_tiles:
                    st = jnp.where(causal_t, st, -jnp.inf)
                m = m_ref[c]
                m_new = jnp.maximum(m, jnp.max(st, axis=0, keepdims=True))
                p = jnp.exp2(st - m_new).astype(BF16)
                vt = jnp.concatenate([vt_ref[0, h * MLA_V:(h + 1) * MLA_V, pl.ds(k0, TQ)], ones_rows], axis=0)
                upd.append((c, m_new, jnp.exp2(m - m_new), _dot(vt, p)))
            for c, m_new, alpha, pv in upd:
                m_ref[c] = m_new
                acc_ref[c] = alpha * acc_ref[c] + pv

        def body(kj, carry2):
            kv_step(pl.multiple_of(kj * TQ, TQ), (0, 1), ())
            return carry2

        lax.fori_loop(0, 2 * a, body, 0)
        kv_step(q0s[0], (0, 1), (0,))
        kv_step(q0s[1], (1,), (1,))

        for t in range(2):
            for pr in range(MLA_HPS // 2):
                halves = []
                for h in (2 * pr, 2 * pr + 1):
                    acc = acc_ref[t * MLA_HPS + h]
                    o = acc[0:MLA_V] / acc[MLA_V:MLA_V + 1]
                    ms = jnp.mean(o * o, axis=0, keepdims=True)
                    halves.append(o * lax.rsqrt(ms + EPS))
                y = jnp.transpose(jnp.concatenate(halves, axis=0))
                y = y * g_ref[:, pr * LANES:(pr + 1) * LANES]
                o_ref[0, pl.ds(q0s[t], TQ), pr * LANES:(pr + 1) * LANES] = y.astype(o_ref.dtype)
        return carry

    lax.fori_loop(0, s // (2 * TQ), q_pair, 0)


def _mla(qa, ka, vt, g_out):
    bsz, s, _ = qa.shape
    groups = MLA_HEADS // MLA_HPS
    return pl.pallas_call(
        _mla_kernel,
        out_shape=jax.ShapeDtypeStruct((bsz, s, MLA_OUT), BF16),
        grid=(bsz, groups),
        in_specs=[
            pl.BlockSpec((1, s, MLA_HPS * HEAD_PAD), lambda b, p: (b, 0, p)),
            pl.BlockSpec((1, s, MLA_HPS * HEAD_PAD), lambda b, p: (b, 0, p)),
            pl.BlockSpec((1, MLA_HPS * MLA_V, s), lambda b, p: (b, p, 0)),
            pl.BlockSpec((1, MLA_HPS * MLA_V), lambda b, p: (0, p)),
        ],
        out_specs=pl.BlockSpec((1, s, MLA_HPS * MLA_V), lambda b, p: (b, 0, p)),
        scratch_shapes=[pltpu.VMEM((2 * MLA_HPS, 1, TQ), F32),
                        pltpu.VMEM((2 * MLA_HPS, MLA_V + ONES_ROWS, TQ), F32)],
        compiler_params=pltpu.CompilerParams(
            dimension_semantics=("parallel", "parallel"), vmem_limit_bytes=VMEM_LIMIT),
        name="mla_attention",
    )(qa, ka, vt, g_out)


def _split3(x):
    hi = x.astype(BF16)
    r1 = x - hi.astype(F32)
    mid = r1.astype(BF16)
    lo = (r1 - mid.astype(F32)).astype(BF16)
    return hi, mid, lo


def _mlstm_kernel(qt_ref, k_ref, vt_ref, og_ref, gcol_ref, grow_ref, g_ref, o_ref, cn_ref, m_ref):
    L = k_ref.shape[1]
    ci = pl.program_id(1)

    @pl.when(ci == 0)
    def _():
        cn_ref[...] = jnp.zeros(cn_ref.shape, F32)
        m_ref[...] = jnp.zeros(m_ref.shape, F32)

    row = _row_iota((L, L))
    col = _lane_iota((L, L))
    upper = row <= col
    tri = jnp.where(row >= col, 1.0, 0.0).astype(BF16)
    tri_t = jnp.where(upper, 1.0, 0.0).astype(BF16)

    gcol = gcol_ref[0]
    grow = grow_ref[0]
    ccol = sum(_dot(tri, part) for part in _split3(gcol))
    crow = sum(_dot(part, tri_t) for part in _split3(grow))

    lane1 = _lane_iota((1, LANES))
    head_lane = (lane1 >= M_HEADS) & (lane1 < 2 * M_HEADS)
    c_all = ccol - pltpu.roll(gcol, M_HEADS, axis=1)
    btot_all = ccol[L - 1:L, :]
    m0_all = m_ref[...]
    g_all = btot_all - c_all
    m_loc_all = jnp.max(g_all, axis=0, keepdims=True)
    wgt_all = jnp.exp(g_all - m_loc_all)
    m_new_all = jnp.where(head_lane, jnp.maximum(btot_all + m0_all, m_loc_all), 0.0)
    a_all = jnp.exp(btot_all + m0_all - m_new_all)
    e_all = jnp.exp(m_loc_all - m_new_all)
    m_ref[...] = m_new_all

    lane128 = _lane_iota((L, LANES))
    ones_rows = jnp.ones((ONES_ROWS, L), BF16)

    def head_keys(h):
        pr, half = divmod(h, 2)
        kp = k_ref[0, :, pr * LANES:(pr + 1) * LANES]
        in_half = (lane128 >= half * M_DK) & (lane128 < (half + 1) * M_DK)
        return kp, in_half

    sts = []
    for h in range(M_HEADS):
        kp, in_half = head_keys(h)
        kh = jnp.where(in_half, kp, jnp.zeros_like(kp))
        sts.append(_dot(kh, qt_ref[0, (h // 2) * LANES:(h // 2 + 1) * LANES, :]))

    for h in range(M_HEADS):
        hl = M_HEADS + h
        bc_row = crow[hl:hl + 1, :]
        m0 = m0_all[:, hl:hl + 1]
        log_d = jnp.where(upper, bc_row - c_all[:, hl:hl + 1], -jnp.inf)
        m_d = jnp.max(log_d, axis=0, keepdims=True)
        log_inter = bc_row + m0
        m_t = jnp.maximum(log_inter, m_d)
        d_m = jnp.exp(log_d - m_t)
        inter = jnp.exp(log_inter - m_t)
        sc = (sts[h] * d_m).astype(BF16)
        vaug = jnp.concatenate([vt_ref[0, h * M_DV:(h + 1) * M_DV, :], ones_rows], axis=0)
        intra = _dot(vaug, sc)
        cn = cn_ref[h]
        cross = _dot(cn.astype(BF16), qt_ref[0, (h // 2) * LANES:(h // 2 + 1) * LANES, :])
        num = intra[0:M_DV] + inter * cross[0:M_DV]
        den = intra[M_DV:M_DV + 1] + inter * cross[M_DV:M_DV + 1]
        hh = jnp.transpose(num / jnp.maximum(jnp.abs(den), jnp.exp(-m_t)))
        y = og_ref[0, :, h * M_DV:(h + 1) * M_DV].astype(F32) * hh
        y = _rms(y, g_ref[:, h * M_DV:(h + 1) * M_DV])
        o_ref[0, :, h * M_DV:(h + 1) * M_DV] = y.astype(o_ref.dtype)

        kp, in_half = head_keys(h)
        kw = (jnp.where(in_half, kp.astype(F32), 0.0) * wgt_all[:, hl:hl + 1]).astype(BF16)
        d_cn = _dot(vaug, kw)
        cn_ref[h] = a_all[:, hl:hl + 1] * cn + e_all[:, hl:hl + 1] * d_cn


def _mlstm(qmt, km, vmt, og, gcol, grow, g_out):
    bsz, s, _ = km.shape
    L = M_CHUNK
    tok = lambda w: pl.BlockSpec((1, L, w), lambda b, c: (b, c, 0))
    tok_t = lambda w: pl.BlockSpec((1, w, L), lambda b, c: (b, 0, c))
    return pl.pallas_call(
        _mlstm_kernel,
        out_shape=jax.ShapeDtypeStruct((bsz, s, M_OUT), BF16),
        grid=(bsz, s // L),
        in_specs=[
            tok_t(M_QK), tok(M_QK), tok_t(M_OUT), tok(M_OUT), tok(LANES), tok_t(SUBLANES),
            pl.BlockSpec((1, M_OUT), lambda b, c: (0, 0)),
        ],
        out_specs=tok(M_OUT),
        scratch_shapes=[pltpu.VMEM((M_HEADS, M_DV + ONES_ROWS, LANES), F32),
                        pltpu.VMEM((1, LANES), F32)],
        compiler_params=pltpu.CompilerParams(
            dimension_semantics=("parallel", "arbitrary"), vmem_limit_bytes=VMEM_LIMIT),
        name="mlstm",
    )(qmt, km, vmt, og, gcol, grow, g_out)


def _outffn_kernel(x_ref, ya_ref, yb_ref, mod_ref, wout_ref, gffn_ref, wg_ref, wu_ref, wd_ref, gfin_ref,
                   o_ref, *, final_norm):
    x = x_ref[0]
    gate_a = mod_ref[0, 2:3, :]
    shift_f = mod_ref[0, 3:4, :]
    scale_f = mod_ref[0, 4:5, :]
    gate_f = mod_ref[0, 5:6, :]

    mix = _dot(ya_ref[0], wout_ref[0:MLA_OUT, :]) + _dot(yb_ref[0], wout_ref[MLA_OUT:MLA_OUT + M_OUT, :])
    x1 = x + gate_a * mix
    hb = (_rms(x1, gffn_ref[...] * (1.0 + scale_f)) + shift_f).astype(BF16)

    d_ff = wg_ref.shape[1]
    step = -(-d_ff // FF_CHUNKS // (2 * LANES)) * (2 * LANES)
    ffn = None
    for c0 in range(0, d_ff, step):
        c1 = min(c0 + step, d_ff)
        gt = _dot(hb, wg_ref[:, c0:c1])
        up = _dot(hb, wu_ref[:, c0:c1])
        act = (gt * _sigmoid(gt) * up).astype(BF16)
        part = _dot(act, wd_ref[c0:c1, :])
        ffn = part if ffn is None else ffn + part
    x2 = x1 + gate_f * ffn
    if final_norm:
        x2 = _rms(x2, gfin_ref[...])
    o_ref[0] = x2


def _outffn(x, ya, yb, mod, wout, gffn, wg, wu, wd, gfin, final_norm):
    bsz, s, d = x.shape
    tm = TM_OUT
    tok = lambda w: pl.BlockSpec((1, tm, w), lambda b, i: (b, i, 0))
    return pl.pallas_call(
        functools.partial(_outffn_kernel, final_norm=final_norm),
        out_shape=jax.ShapeDtypeStruct((bsz, s, d), F32),
        grid=(bsz, s // tm),
        in_specs=[
            tok(d), tok(MLA_OUT), tok(M_OUT),
            pl.BlockSpec((1, 6, d), lambda b, i: (b, 0, 0)),
            _const_spec(wout.shape),
            _const_spec((1, d)),
            _const_spec(wg.shape), _const_spec(wu.shape), _const_spec(wd.shape),
            _const_spec((1, d)),
        ],
        out_specs=tok(d),
        compiler_params=pltpu.CompilerParams(
            dimension_semantics=("parallel", "parallel"), vmem_limit_bytes=VMEM_LIMIT),
        name="outproj_ffn",
    )(x, ya, yb, mod, wout, gffn, wg, wu, wd, gfin)


def _prep_w_in(w_in):
    d = w_in.shape[0]
    o_q, o_kv, o_kr = 0, MLA_Q_RANK, MLA_Q_RANK + MLA_KV_RANK
    o_qk = o_kr + MLA_ROPE
    o_v = o_qk + 2 * M_QK
    o_o = o_v + M_OUT
    o_i = o_o + M_OUT
    o_f = o_i + M_HEADS
    small = jnp.concatenate([
        w_in[:, o_i:o_i + M_HEADS], w_in[:, o_f:o_f + M_HEADS],
        jnp.zeros((d, ROPE_LO - 2 * M_HEADS), w_in.dtype),
        w_in[:, o_kr:o_kr + MLA_ROPE],
        jnp.zeros((d, HEAD_PAD - ROPE_HI), w_in.dtype)], axis=1)
    return jnp.concatenate([w_in[:, o_q:o_kr], w_in[:, o_qk:o_i], small], axis=1).astype(BF16)


def _prep_w_uq(w_uq):
    r = w_uq.shape[0]
    w = w_uq.reshape(r, MLA_HEADS, MLA_NOPE + MLA_ROPE)
    w = jnp.pad(w, ((0, 0), (0, 0), (0, HEAD_PAD - MLA_NOPE - MLA_ROPE)))
    return w.reshape(r, QK_PAD).astype(BF16)


def _prep_w_ukv(w_ukv):
    r = w_ukv.shape[0]
    w = w_ukv.reshape(r, MLA_HEADS, MLA_NOPE + MLA_V)
    wk = jnp.pad(w[:, :, :MLA_NOPE], ((0, 0), (0, 0), (0, HEAD_PAD - MLA_NOPE))).reshape(r, QK_PAD)
    wv = w[:, :, MLA_NOPE:].reshape(r, MLA_OUT)
    return wk.astype(BF16), wv.astype(BF16)


def kernel(x, c, positions, w_ada, b_ada, g_mix, w_in, g_q, w_uq, g_kv, w_ukv, conv_w, conv_b, b_gates,
           g_out_mla, g_out_mlstm, w_out, g_ffn, w_gate, w_up, w_down, g_final):
    bsz, s, d = x.shape
    depth = w_ada.shape[0]
    cos_blk, sin_blk = _rope_tables(positions)
    row = lambda v: v.reshape(1, -1)
    for l in range(depth):
        mod = _adaln(c, w_ada[l], b_ada[l]).reshape(bsz, 6, d)
        wuk_p, wv_p = _prep_w_ukv(w_ukv[l])
        bg_blk = jnp.pad(b_gates[l], (0, LANES - 2 * M_HEADS)).reshape(1, LANES)
        qa, ka, vt, qmt, km, vmt, og, gcol, grow = _inproj(
            x, mod, row(g_mix[l]), _prep_w_in(w_in[l]), row(g_q[l]), _prep_w_uq(w_uq[l]),
            row(g_kv[l]), wuk_p, wv_p, conv_w[l], row(conv_b[l]), bg_blk, cos_blk, sin_blk)
        ya = _mla(qa, ka, vt, row(g_out_mla[l]))
        yb = _mlstm(qmt, km, vmt, og, gcol, grow, row(g_out_mlstm[l]))
        x = _outffn(x, ya, yb, mod, w_out[l].astype(BF16), row(g_ffn[l]),
                    w_gate[l].astype(BF16), w_up[l].astype(BF16), w_down[l].astype(BF16),
                    row(g_final), final_norm=(l == depth - 1))
    return x
```

```python
import functools

import numpy as np
import jax
import jax.numpy as jnp
from jax import lax
from jax.experimental import pallas as pl
from jax.experimental.pallas import tpu as pltpu

F32 = jnp.float32
BF16 = jnp.bfloat16

LANES = 128
SUBLANES = 8

MLA_HEADS = 8
MLA_NOPE = 64
MLA_ROPE = 32
MLA_V = 64
MLA_Q_RANK = 384
MLA_KV_RANK = 256
ROPE_THETA = 10000.0
ROPE_HALF = MLA_ROPE // 2
M_HEADS = 4
M_DK = 64
M_DV = 128
CONV_W = 4
EPS = 1e-6
LOG2E = 1.4426950408889634

MLA_OUT = MLA_HEADS * MLA_V
M_OUT = M_HEADS * M_DV
M_QK = M_HEADS * M_DK
HEAD_PAD = LANES
QK_PAD = MLA_HEADS * HEAD_PAD
ROPE_LO = MLA_NOPE
ROPE_MID = MLA_NOPE + ROPE_HALF
ROPE_HI = MLA_NOPE + MLA_ROPE

C_Q = 0
C_KV = C_Q + MLA_Q_RANK
C_QK = C_KV + MLA_KV_RANK
C_V = C_QK + 2 * M_QK
C_O = C_V + M_OUT
C_S = C_O + M_OUT
D_IN_PAD = C_S + LANES

TM_ROPE = 1024
TM_IN = 1024
TM_SUB = 512
TQ = 512
MLA_HPS = 4
ONES_ROWS = 16
M_CHUNK = 256
TM_OUT = 512
FF_CHUNKS = 3

VMEM_LIMIT = 56 * 1024 * 1024


def _lane_iota(shape):
    return lax.broadcasted_iota(jnp.int32, shape, len(shape) - 1)


def _row_iota(shape):
    return lax.broadcasted_iota(jnp.int32, shape, len(shape) - 2)


def _dot(a, b):
    return jnp.dot(a, b, preferred_element_type=F32)


def _dot_nt(a, b):
    return lax.dot_general(a, b, (((1,), (1,)), ((), ())), preferred_element_type=F32)


def _log_sigmoid(x):
    return jnp.minimum(x, 0.0) - jnp.log1p(jnp.exp(-jnp.abs(x)))


def _sigmoid(x):
    return 1.0 / (1.0 + jnp.exp(-x))


def _adaln_kernel(c_ref, w_ref, b_ref, o_ref):
    c = c_ref[...]
    cond = c * _sigmoid(c)
    o_ref[...] = _dot(cond, w_ref[...]) + b_ref[...]


def _adaln(c, w, b):
    bsz, d = c.shape
    n = w.shape[1]
    tn = 1536
    return pl.pallas_call(
        _adaln_kernel,
        out_shape=jax.ShapeDtypeStruct((bsz, n), F32),
        grid=(n // tn,),
        in_specs=[
            pl.BlockSpec((bsz, d), lambda j: (0, 0)),
            pl.BlockSpec((d, tn), lambda j: (0, j)),
            pl.BlockSpec((1, tn), lambda j: (0, j)),
        ],
        out_specs=pl.BlockSpec((bsz, tn), lambda j: (0, j)),
        compiler_params=pltpu.CompilerParams(
            dimension_semantics=("parallel",), vmem_limit_bytes=VMEM_LIMIT),
        name="adaln",
    )(c, w, b.reshape(1, n))


def _rope_kernel(pos_ref, inv_ref, cos_ref, sin_ref):
    ang = inv_ref[...] * pos_ref[0].astype(F32)
    c = jnp.cos(ang)
    sn = jnp.sin(ang)
    t = ang.shape[1]
    lo = jnp.ones((ROPE_LO, t), F32)
    hi = jnp.ones((HEAD_PAD - ROPE_HI, t), F32)
    cos_ref[0] = jnp.transpose(jnp.concatenate([lo, c, c, hi], axis=0))
    sin_ref[0] = jnp.transpose(jnp.concatenate([0.0 * lo, -sn, sn, 0.0 * hi], axis=0))


def _rope_tables(positions):
    bsz, s = positions.shape
    t = TM_ROPE
    inv = ROPE_THETA ** (-np.arange(ROPE_HALF, dtype=np.float64) / ROPE_HALF)
    inv = jnp.asarray(inv.astype(np.float32).reshape(ROPE_HALF, 1))
    out = jax.ShapeDtypeStruct((bsz, s, HEAD_PAD), F32)
    blk = pl.BlockSpec((1, t, HEAD_PAD), lambda b, i: (b, i, 0))
    return pl.pallas_call(
        _rope_kernel,
        out_shape=(out, out),
        grid=(bsz, s // t),
        in_specs=[
            pl.BlockSpec((1, 1, t), lambda b, i: (b, 0, i)),
            pl.BlockSpec((ROPE_HALF, 1), lambda b, i: (0, 0)),
        ],
        out_specs=(blk, blk),
        compiler_params=pltpu.CompilerParams(
            dimension_semantics=("parallel", "parallel"), vmem_limit_bytes=VMEM_LIMIT),
        name="rope_tables",
    )(positions.reshape(bsz, 1, s), inv)


def _rms(x, g):
    return x * lax.rsqrt(jnp.mean(x * x, axis=-1, keepdims=True) + EPS) * g


def _rope_block(x, cos_blk, sin_blk, lane):
    fwd = pltpu.roll(x, HEAD_PAD - ROPE_HALF, axis=1)
    bwd = pltpu.roll(x, ROPE_HALF, axis=1)
    rot = jnp.where(lane < ROPE_MID, fwd, bwd)
    return x * cos_blk + rot * sin_blk


def _inproj_kernel(x_ref, mod_ref, gmix_ref, win_ref, gq_ref, wuq_ref, gkv_ref, wuk_ref, wv_ref,
                   convw_ref, convb_ref, bg_ref, cos_ref, sin_ref,
                   qa_ref, ka_ref, vt_ref, qmt_ref, km_ref, vmt_ref, og_ref, gcol_ref, grow_ref,
                   zbuf_ref):
    tm = x_ref.shape[1]
    si = pl.program_id(1)
    shift = mod_ref[0, 0:1, :]
    gain = gmix_ref[...] * (1.0 + mod_ref[0, 1:2, :])
    lane = _lane_iota((TM_SUB, LANES))

    @pl.when(si == 0)
    def _():
        zbuf_ref[0:SUBLANES, :] = jnp.zeros((SUBLANES, 2 * M_QK), F32)

    for r0 in range(0, tm, TM_SUB):
        rows = slice(r0, r0 + TM_SUB)
        hb = (_rms(x_ref[0, rows, :], gain) + shift).astype(BF16)
        cos_blk = cos_ref[0, rows, :]
        sin_blk = sin_ref[0, rows, :]

        ql = _dot(hb, win_ref[:, C_Q:C_KV])
        qn = _rms(ql, gq_ref[...]).astype(BF16)
        qa = _dot(qn, wuq_ref[...]) * ((MLA_NOPE + MLA_ROPE) ** -0.5 * LOG2E)
        for hd in range(MLA_HEADS):
            blk = qa[:, hd * HEAD_PAD:(hd + 1) * HEAD_PAD]
            qa_ref[0, rows, hd * HEAD_PAD:(hd + 1) * HEAD_PAD] = (
                _rope_block(blk, cos_blk, sin_blk, lane).astype(BF16))

        zs = _dot(hb, win_ref[:, C_S:D_IN_PAD])
        kr = _rope_block(zs, cos_blk, sin_blk, lane)
        kr = jnp.where((lane >= ROPE_LO) & (lane < ROPE_HI), kr, 0.0)
        gpre = zs + bg_ref[...]
        gates = jnp.where(lane < M_HEADS, gpre, _log_sigmoid(gpre))
        gcol_ref[0, rows, :] = gates
        grow_ref[0, :, rows] = jnp.transpose(gates)[0:SUBLANES, :]

        kl = _dot(hb, win_ref[:, C_KV:C_QK])
        kn = _rms(kl, gkv_ref[...]).astype(BF16)
        ka = _dot(kn, wuk_ref[...])
        for hd in range(MLA_HEADS):
            ka_ref[0, rows, hd * HEAD_PAD:(hd + 1) * HEAD_PAD] = (
                ka[:, hd * HEAD_PAD:(hd + 1) * HEAD_PAD] + kr).astype(BF16)
        vt_ref[0, :, rows] = jnp.transpose(_dot(kn, wv_ref[...])).astype(BF16)

        zqk = _dot(hb, win_ref[:, C_QK:C_V])
        z0 = SUBLANES + r0
        zbuf_ref[z0:z0 + TM_SUB, :] = zqk
        acc = zqk * convw_ref[CONV_W - 1:CONV_W, :] + convb_ref[...]
        for j in range(1, CONV_W):
            acc = acc + zbuf_ref[z0 - j:z0 - j + TM_SUB, :] * convw_ref[CONV_W - 1 - j:CONV_W - j, :]
        qk = acc * _sigmoid(acc)
        qmt_ref[0, :, rows] = jnp.transpose(qk[:, 0:M_QK]).astype(BF16)
        km_ref[0, rows, :] = (qk[:, M_QK:2 * M_QK] * (M_DK ** -0.5)).astype(BF16)

        vmt_ref[0, :, rows] = jnp.transpose(_dot(hb, win_ref[:, C_V:C_O])).astype(BF16)
        og_ref[0, rows, :] = _sigmoid(_dot(hb, win_ref[:, C_O:C_S])).astype(BF16)

    zbuf_ref[0:SUBLANES, :] = zbuf_ref[tm:tm + SUBLANES, :]


def _const_spec(shape):
    nd = len(shape)
    return pl.BlockSpec(shape, lambda *_: (0,) * nd, pipeline_mode=pl.Buffered(1))


def _inproj(x, mod, gmix, win_p, gq, wuq_p, gkv, wuk_p, wv_p, convw, convb, bg_blk, cos_blk, sin_blk):
    bsz, s, d = x.shape
    tm = TM_IN
    tok = lambda w: pl.BlockSpec((1, tm, w), lambda b, i: (b, i, 0))
    out_shapes = (
        jax.ShapeDtypeStruct((bsz, s, QK_PAD), BF16),
        jax.ShapeDtypeStruct((bsz, s, QK_PAD), BF16),
        jax.ShapeDtypeStruct((bsz, MLA_OUT, s), BF16),
        jax.ShapeDtypeStruct((bsz, M_QK, s), BF16),
        jax.ShapeDtypeStruct((bsz, s, M_QK), BF16),
        jax.ShapeDtypeStruct((bsz, M_OUT, s), BF16),
        jax.ShapeDtypeStruct((bsz, s, M_OUT), BF16),
        jax.ShapeDtypeStruct((bsz, s, LANES), F32),
        jax.ShapeDtypeStruct((bsz, SUBLANES, s), F32),
    )
    out_specs = (
        tok(QK_PAD), tok(QK_PAD),
        pl.BlockSpec((1, MLA_OUT, tm), lambda b, i: (b, 0, i)),
        pl.BlockSpec((1, M_QK, tm), lambda b, i: (b, 0, i)),
        tok(M_QK),
        pl.BlockSpec((1, M_OUT, tm), lambda b, i: (b, 0, i)),
        tok(M_OUT), tok(LANES),
        pl.BlockSpec((1, SUBLANES, tm), lambda b, i: (b, 0, i)),
    )
    in_specs = [
        tok(d),
        pl.BlockSpec((1, 6, d), lambda b, i: (b, 0, 0)),
        _const_spec((1, d)),
        _const_spec(win_p.shape),
        _const_spec((1, MLA_Q_RANK)),
        _const_spec(wuq_p.shape),
        _const_spec((1, MLA_KV_RANK)),
        _const_spec(wuk_p.shape),
        _const_spec(wv_p.shape),
        _const_spec((CONV_W, 2 * M_QK)),
        _const_spec((1, 2 * M_QK)),
        _const_spec((1, LANES)),
        tok(LANES), tok(LANES),
    ]
    return pl.pallas_call(
        _inproj_kernel,
        out_shape=out_shapes,
        grid=(bsz, s // tm),
        in_specs=in_specs,
        out_specs=out_specs,
        scratch_shapes=[pltpu.VMEM((tm + SUBLANES, 2 * M_QK), F32)],
        compiler_params=pltpu.CompilerParams(
            dimension_semantics=("parallel", "arbitrary"), vmem_limit_bytes=VMEM_LIMIT),
        name="inproj",
    )(x, mod, gmix, win_p, gq, wuq_p, gkv, wuk_p, wv_p, convw, convb, bg_blk, cos_blk, sin_blk)


def _mla_kernel(q_ref, k_ref, vt_ref, g_ref, o_ref, m_ref, acc_ref):
    s = q_ref.shape[1]
    causal_t = _row_iota((TQ, TQ)) <= _lane_iota((TQ, TQ))
    ones_rows = jnp.ones((ONES_ROWS, TQ), BF16)

    def q_pair(a, carry):
        q0s = [pl.multiple_of((2 * a + t) * TQ, TQ) for t in range(2)]
        qs = [[q_ref[0, pl.ds(q0s[t], TQ), h * HEAD_PAD:(h + 1) * HEAD_PAD] for h in range(MLA_HPS)]
              for t in range(2)]
        for c in range(2 * MLA_HPS):
            m_ref[c] = jnp.full((1, TQ), -jnp.inf, F32)
            acc_ref[c] = jnp.zeros((MLA_V + ONES_ROWS, TQ), F32)

        def kv_step(k0, tiles, masked_tiles):
            chains = [(t, h) for t in tiles for h in range(MLA_HPS)]
            ks = [k_ref[0, pl.ds(k0, TQ), h * HEAD_PAD:(h + 1) * HEAD_PAD] for h in range(MLA_HPS)]
            sts = [_dot_nt(ks[h], qs[t][h]) for t, h in chains]
            upd = []
            for (t, h), st in zip(chains, sts):
                c = t * MLA_HPS + h
                if t in masked_tiles:
                    st = jnp.where(causal_t, st, -jnp.inf)
                m = m_ref[c]
                m_new = jnp.maximum(m, jnp.max(st, axis=0, keepdims=True))
                p = jnp.exp2(st - m_new).astype(BF16)
                vt = jnp.concatenate([vt_ref[0, h * MLA_V:(h + 1) * MLA_V, pl.ds(k0, TQ)], ones_rows], axis=0)
                upd.append((c, m_new, jnp.exp2(m - m_new), _dot(vt, p)))
            for c, m_new, alpha, pv in upd:
                m_ref[c] = m_new
                acc_ref[c] = alpha * acc_ref[c] + pv

        def body(kj, carry2):
            kv_step(pl.multiple_of(kj * TQ, TQ), (0, 1), ())
            return carry2

        lax.fori_loop(0, 2 * a, body, 0)
        kv_step(q0s[0], (0, 1), (0,))
        kv_step(q0s[1], (1,), (1,))

        for t in range(2):
            for pr in range(MLA_HPS // 2):
                halves = []
                for h in (2 * pr, 2 * pr + 1):
                    acc = acc_ref[t * MLA_HPS + h]
                    o = acc[0:MLA_V] / acc[MLA_V:MLA_V + 1]
                    ms = jnp.mean(o * o, axis=0, keepdims=True)
                    halves.append(o * lax.rsqrt(ms + EPS))
                y = jnp.transpose(jnp.concatenate(halves, axis=0))
                y = y * g_ref[:, pr * LANES:(pr + 1) * LANES]
                o_ref[0, pl.ds(q0s[t], TQ), pr * LANES:(pr + 1) * LANES] = y.astype(o_ref.dtype)
        return carry

    lax.fori_loop(0, s // (2 * TQ), q_pair, 0)


def _mla(qa, ka, vt, g_out):
    bsz, s, _ = qa.shape
    groups = MLA_HEADS // MLA_HPS
    return pl.pallas_call(
        _mla_kernel,
        out_shape=jax.ShapeDtypeStruct((bsz, s, MLA_OUT), BF16),
        grid=(bsz, groups),
        in_specs=[
            pl.BlockSpec((1, s, MLA_HPS * HEAD_PAD), lambda b, p: (b, 0, p)),
            pl.BlockSpec((1, s, MLA_HPS * HEAD_PAD), lambda b, p: (b, 0, p)),
            pl.BlockSpec((1, MLA_HPS * MLA_V, s), lambda b, p: (b, p, 0)),
            pl.BlockSpec((1, MLA_HPS * MLA_V), lambda b, p: (0, p)),
        ],
        out_specs=pl.BlockSpec((1, s, MLA_HPS * MLA_V), lambda b, p: (b, 0, p)),
        scratch_shapes=[pltpu.VMEM((2 * MLA_HPS, 1, TQ), F32),
                        pltpu.VMEM((2 * MLA_HPS, MLA_V + ONES_ROWS, TQ), F32)],
        compiler_params=pltpu.CompilerParams(
            dimension_semantics=("parallel", "parallel"), vmem_limit_bytes=VMEM_LIMIT),
        name="mla_attention",
    )(qa, ka, vt, g_out)


def _split3(x):
    hi = x.astype(BF16)
    r1 = x - hi.astype(F32)
    mid = r1.astype(BF16)
    lo = (r1 - mid.astype(F32)).astype(BF16)
    return hi, mid, lo


def _mlstm_kernel(qt_ref, k_ref, vt_ref, og_ref, gcol_ref, grow_ref, g_ref, o_ref, cn_ref, m_ref):
    L = k_ref.shape[1]
    ci = pl.program_id(1)

    @pl.when(ci == 0)
    def _():
        cn_ref[...] = jnp.zeros(cn_ref.shape, F32)
        m_ref[...] = jnp.zeros(m_ref.shape, F32)

    row = _row_iota((L, L))
    col = _lane_iota((L, L))
    upper = row <= col
    tri = jnp.where(row >= col, 1.0, 0.0).astype(BF16)
    tri_t = jnp.where(upper, 1.0, 0.0).astype(BF16)

    gcol = gcol_ref[0]
    grow = grow_ref[0]
    ccol = sum(_dot(tri, part) for part in _split3(gcol))
    crow = sum(_dot(part, tri_t) for part in _split3(grow))

    lane1 = _lane_iota((1, LANES))
    head_lane = (lane1 >= M_HEADS) & (lane1 < 2 * M_HEADS)
    c_all = ccol - pltpu.roll(gcol, M_HEADS, axis=1)
    btot_all = ccol[L - 1:L, :]
    m0_all = m_ref[...]
    g_all = btot_all - c_all
    m_loc_all = jnp.max(g_all, axis=0, keepdims=True)
    wgt_all = jnp.exp(g_all - m_loc_all)
    m_new_all = jnp.where(head_lane, jnp.maximum(btot_all + m0_all, m_loc_all), 0.0)
    a_all = jnp.exp(btot_all + m0_all - m_new_all)
    e_all = jnp.exp(m_loc_all - m_new_all)
    m_ref[...] = m_new_all

    lane128 = _lane_iota((L, LANES))
    ones_rows = jnp.ones((ONES_ROWS, L), BF16)

    def head_keys(h):
        pr, half = divmod(h, 2)
        kp = k_ref[0, :, pr * LANES:(pr + 1) * LANES]
        in_half = (lane128 >= half * M_DK) & (lane128 < (half + 1) * M_DK)
        return kp, in_half

    sts = []
    for h in range(M_HEADS):
        kp, in_half = head_keys(h)
        kh = jnp.where(in_half, kp, jnp.zeros_like(kp))
        sts.append(_dot(kh, qt_ref[0, (h // 2) * LANES:(h // 2 + 1) * LANES, :]))

    for h in range(M_HEADS):
        hl = M_HEADS + h
        bc_row = crow[hl:hl + 1, :]
        m0 = m0_all[:, hl:hl + 1]
        log_d = jnp.where(upper, bc_row - c_all[:, hl:hl + 1], -jnp.inf)
        m_d = jnp.max(log_d, axis=0, keepdims=True)
        log_inter = bc_row + m0
        m_t = jnp.maximum(log_inter, m_d)
        d_m = jnp.exp(log_d - m_t)
        inter = jnp.exp(log_inter - m_t)
        sc = (sts[h] * d_m).astype(BF16)
        vaug = jnp.concatenate([vt_ref[0, h * M_DV:(h + 1) * M_DV, :], ones_rows], axis=0)
        intra = _dot(vaug, sc)
        cn = cn_ref[h]
        cross = _dot(cn.astype(BF16), qt_ref[0, (h // 2) * LANES:(h // 2 + 1) * LANES, :])
        num = intra[0:M_DV] + inter * cross[0:M_DV]
        den = intra[M_DV:M_DV + 1] + inter * cross[M_DV:M_DV + 1]
        hh = jnp.transpose(num / jnp.maximum(jnp.abs(den), jnp.exp(-m_t)))
        y = og_ref[0, :, h * M_DV:(h + 1) * M_DV].astype(F32) * hh
        y = _rms(y, g_ref[:, h * M_DV:(h + 1) * M_DV])
        o_ref[0, :, h * M_DV:(h + 1) * M_DV] = y.astype(o_ref.dtype)

        kp, in_half = head_keys(h)
        kw = (jnp.where(in_half, kp.astype(F32), 0.0) * wgt_all[:, hl:hl + 1]).astype(BF16)
        d_cn = _dot(vaug, kw)
        cn_ref[h] = a_all[:, hl:hl + 1] * cn + e_all[:, hl:hl + 1] * d_cn


def _mlstm(qmt, km, vmt, og, gcol, grow, g_out):
    bsz, s, _ = km.shape
    L = M_CHUNK
    tok = lambda w: pl.BlockSpec((1, L, w), lambda b, c: (b, c, 0))
    tok_t = lambda w: pl.BlockSpec((1, w, L), lambda b, c: (b, 0, c))
    return pl.pallas_call(
        _mlstm_kernel,
        out_shape=jax.ShapeDtypeStruct((bsz, s, M_OUT), BF16),
        grid=(bsz, s // L),
        in_specs=[
            tok_t(M_QK), tok(M_QK), tok_t(M_OUT), tok(M_OUT), tok(LANES), tok_t(SUBLANES),
            pl.BlockSpec((1, M_OUT), lambda b, c: (0, 0)),
        ],
        out_specs=tok(M_OUT),
        scratch_shapes=[pltpu.VMEM((M_HEADS, M_DV + ONES_ROWS, LANES), F32),
                        pltpu.VMEM((1, LANES), F32)],
        compiler_params=pltpu.CompilerParams(
            dimension_semantics=("parallel", "arbitrary"), vmem_limit_bytes=VMEM_LIMIT),
        name="mlstm",
    )(qmt, km, vmt, og, gcol, grow, g_out)


def _outffn_kernel(x_ref, ya_ref, yb_ref, mod_ref, wout_ref, gffn_ref, wg_ref, wu_ref, wd_ref, gfin_ref,
                   o_ref, *, final_norm):
    x = x_ref[0]
    gate_a = mod_ref[0, 2:3, :]
    shift_f = mod_ref[0, 3:4, :]
    scale_f = mod_ref[0, 4:5, :]
    gate_f = mod_ref[0, 5:6, :]

    mix = _dot(ya_ref[0], wout_ref[0:MLA_OUT, :]) + _dot(yb_ref[0], wout_ref[MLA_OUT:MLA_OUT + M_OUT, :])
    x1 = x + gate_a * mix
    hb = (_rms(x1, gffn_ref[...] * (1.0 + scale_f)) + shift_f).astype(BF16)

    d_ff = wg_ref.shape[1]
    step = -(-d_ff // FF_CHUNKS // (2 * LANES)) * (2 * LANES)
    ffn = None
    for c0 in range(0, d_ff, step):
        c1 = min(c0 + step, d_ff)
        gt = _dot(hb, wg_ref[:, c0:c1])
        up = _dot(hb, wu_ref[:, c0:c1])
        act = (gt * _sigmoid(gt) * up).astype(BF16)
        part = _dot(act, wd_ref[c0:c1, :])
        ffn = part if ffn is None else ffn + part
    x2 = x1 + gate_f * ffn
    if final_norm:
        x2 = _rms(x2, gfin_ref[...])
    o_ref[0] = x2


def _outffn(x, ya, yb, mod, wout, gffn, wg, wu, wd, gfin, final_norm):
    bsz, s, d = x.shape
    tm = TM_OUT
    tok = lambda w: pl.BlockSpec((1, tm, w), lambda b, i: (b, i, 0))
    return pl.pallas_call(
        functools.partial(_outffn_kernel, final_norm=final_norm),
        out_shape=jax.ShapeDtypeStruct((bsz, s, d), F32),
        grid=(bsz, s // tm),
        in_specs=[
            tok(d), tok(MLA_OUT), tok(M_OUT),
            pl.BlockSpec((1, 6, d), lambda b, i: (b, 0, 0)),
            _const_spec(wout.shape),
            _const_spec((1, d)),
            _const_spec(wg.shape), _const_spec(wu.shape), _const_spec(wd.shape),
            _const_spec((1, d)),
        ],
        out_specs=tok(d),
        compiler_params=pltpu.CompilerParams(
            dimension_semantics=("parallel", "parallel"), vmem_limit_bytes=VMEM_LIMIT),
        name="outproj_ffn",
    )(x, ya, yb, mod, wout, gffn, wg, wu, wd, gfin)


def _prep_w_in(w_in):
    d = w_in.shape[0]
    o_q, o_kv, o_kr = 0, MLA_Q_RANK, MLA_Q_RANK + MLA_KV_RANK
    o_qk = o_kr + MLA_ROPE
    o_v = o_qk + 2 * M_QK
    o_o = o_v + M_OUT
    o_i = o_o + M_OUT
    o_f = o_i + M_HEADS
    small = jnp.concatenate([
        w_in[:, o_i:o_i + M_HEADS], w_in[:, o_f:o_f + M_HEADS],
        jnp.zeros((d, ROPE_LO - 2 * M_HEADS), w_in.dtype),
        w_in[:, o_kr:o_kr + MLA_ROPE],
        jnp.zeros((d, HEAD_PAD - ROPE_HI), w_in.dtype)], axis=1)
    return jnp.concatenate([w_in[:, o_q:o_kr], w_in[:, o_qk:o_i], small], axis=1).astype(BF16)


def _prep_w_uq(w_uq):
    r = w_uq.shape[0]
    w = w_uq.reshape(r, MLA_HEADS, MLA_NOPE + MLA_ROPE)
    w = jnp.pad(w, ((0, 0), (0, 0), (0, HEAD_PAD - MLA_NOPE - MLA_ROPE)))
    return w.reshape(r, QK_PAD).astype(BF16)


def _prep_w_ukv(w_ukv):
    r = w_ukv.shape[0]
    w = w_ukv.reshape(r, MLA_HEADS, MLA_NOPE + MLA_V)
    wk = jnp.pad(w[:, :, :MLA_NOPE], ((0, 0), (0, 0), (0, HEAD_PAD - MLA_NOPE))).reshape(r, QK_PAD)
    wv = w[:, :, MLA_NOPE:].reshape(r, MLA_OUT)
    return wk.astype(BF16), wv.astype(BF16)


def kernel(x, c, positions, w_ada, b_ada, g_mix, w_in, g_q, w_uq, g_kv, w_ukv, conv_w, conv_b, b_gates,
           g_out_mla, g_out_mlstm, w_out, g_ffn, w_gate, w_up, w_down, g_final):
    bsz, s, d = x.shape
    depth = w_ada.shape[0]
    cos_blk, sin_blk = _rope_tables(positions)
    row = lambda v: v.reshape(1, -1)
    for l in range(depth):
        mod = _adaln(c, w_ada[l], b_ada[l]).reshape(bsz, 6, d)
        wuk_p, wv_p = _prep_w_ukv(w_ukv[l])
        bg_blk = jnp.pad(b_gates[l], (0, LANES - 2 * M_HEADS)).reshape(1, LANES)
        qa, ka, vt, qmt, km, vmt, og, gcol, grow = _inproj(
            x, mod, row(g_mix[l]), _prep_w_in(w_in[l]), row(g_q[l]), _prep_w_uq(w_uq[l]),
            row(g_kv[l]), wuk_p, wv_p, conv_w[l], row(conv_b[l]), bg_blk, cos_blk, sin_blk)
        ya = _mla(qa, ka, vt, row(g_out_mla[l]))
        yb = _mlstm(qmt, km, vmt, og, gcol, grow, row(g_out_mlstm[l]))
        x = _outffn(x, ya, yb, mod, w_out[l].astype(BF16), row(g_ffn[l]),
                    w_gate[l].astype(BF16), w_up[l].astype(BF16), w_down[l].astype(BF16),
                    row(g_final), final_norm=(l == depth - 1))
    return x
```

```python
import functools

import numpy as np
import jax
import jax.numpy as jnp
from jax import lax
from jax.experimental import pallas as pl
from jax.experimental.pallas import tpu as pltpu

F32 = jnp.float32
BF16 = jnp.bfloat16

LANES = 128
SUBLANES = 8

MLA_HEADS = 8
MLA_NOPE = 64
MLA_ROPE = 32
MLA_V = 64
MLA_Q_RANK = 384
MLA_KV_RANK = 256
ROPE_THETA = 10000.0
ROPE_HALF = MLA_ROPE // 2
M_HEADS = 4
M_DK = 64
M_DV = 128
CONV_W = 4
EPS = 1e-6
LOG2E = 1.4426950408889634

MLA_OUT = MLA_HEADS * MLA_V
M_OUT = M_HEADS * M_DV
M_QK = M_HEADS * M_DK
HEAD_PAD = LANES
QK_PAD = MLA_HEADS * HEAD_PAD
ROPE_LO = MLA_NOPE
ROPE_MID = MLA_NOPE + ROPE_HALF
ROPE_HI = MLA_NOPE + MLA_ROPE

C_Q = 0
C_KV = C_Q + MLA_Q_RANK
C_QK = C_KV + MLA_KV_RANK
C_V = C_QK + 2 * M_QK
C_O = C_V + M_OUT
C_S = C_O + M_OUT
D_IN_PAD = C_S + LANES

TM_ROPE = 1024
TM_IN = 1024
TM_SUB = 512
TQ = 512
MLA_HPS = 4
MLA_QG = 2
ONES_ROWS = 16
M_CHUNK = 256
M_SEQS = 1
TM_OUT = 1024
TM_OUT_SUB = 512
FF_CHUNKS = 3

VMEM_LIMIT = 56 * 1024 * 1024


def _lane_iota(shape):
    return lax.broadcasted_iota(jnp.int32, shape, len(shape) - 1)


def _row_iota(shape):
    return lax.broadcasted_iota(jnp.int32, shape, len(shape) - 2)


def _dot(a, b):
    return jnp.dot(a, b, preferred_element_type=F32)


def _dot_nt(a, b):
    return lax.dot_general(a, b, (((1,), (1,)), ((), ())), preferred_element_type=F32)


def _log_sigmoid(x):
    return jnp.minimum(x, 0.0) - jnp.log1p(jnp.exp(-jnp.abs(x)))


def _sigmoid(x):
    return 1.0 / (1.0 + jnp.exp(-x))


def _adaln_kernel(c_ref, w_ref, b_ref, o_ref):
    c = c_ref[...]
    cond = c * _sigmoid(c)
    o_ref[...] = _dot(cond, w_ref[...]) + b_ref[...]


def _adaln(c, w, b):
    bsz, d = c.shape
    n = w.shape[1]
    tn = 1536
    return pl.pallas_call(
        _adaln_kernel,
        out_shape=jax.ShapeDtypeStruct((bsz, n), F32),
        grid=(n // tn,),
        in_specs=[
            pl.BlockSpec((bsz, d), lambda j: (0, 0)),
            pl.BlockSpec((d, tn), lambda j: (0, j)),
            pl.BlockSpec((1, tn), lambda j: (0, j)),
        ],
        out_specs=pl.BlockSpec((bsz, tn), lambda j: (0, j)),
        compiler_params=pltpu.CompilerParams(
            dimension_semantics=("parallel",), vmem_limit_bytes=VMEM_LIMIT),
        name="adaln",
    )(c, w, b.reshape(1, n))


def _rope_kernel(pos_ref, inv_ref, cos_ref, sin_ref):
    ang = inv_ref[...] * pos_ref[0].astype(F32)
    c = jnp.cos(ang)
    sn = jnp.sin(ang)
    t = ang.shape[1]
    lo = jnp.ones((ROPE_LO, t), F32)
    hi = jnp.ones((HEAD_PAD - ROPE_HI, t), F32)
    cos_ref[0] = jnp.transpose(jnp.concatenate([lo, c, c, hi], axis=0))
    sin_ref[0] = jnp.transpose(jnp.concatenate([0.0 * lo, -sn, sn, 0.0 * hi], axis=0))


def _rope_tables(positions):
    bsz, s = positions.shape
    t = TM_ROPE
    inv = ROPE_THETA ** (-np.arange(ROPE_HALF, dtype=np.float64) / ROPE_HALF)
    inv = jnp.asarray(inv.astype(np.float32).reshape(ROPE_HALF, 1))
    out = jax.ShapeDtypeStruct((bsz, s, HEAD_PAD), F32)
    blk = pl.BlockSpec((1, t, HEAD_PAD), lambda b, i: (b, i, 0))
    return pl.pallas_call(
        _rope_kernel,
        out_shape=(out, out),
        grid=(bsz, s // t),
        in_specs=[
            pl.BlockSpec((1, 1, t), lambda b, i: (b, 0, i)),
            pl.BlockSpec((ROPE_HALF, 1), lambda b, i: (0, 0)),
        ],
        out_specs=(blk, blk),
        compiler_params=pltpu.CompilerParams(
            dimension_semantics=("parallel", "parallel"), vmem_limit_bytes=VMEM_LIMIT),
        name="rope_tables",
    )(positions.reshape(bsz, 1, s), inv)


def _rms(x, g):
    return x * lax.rsqrt(jnp.mean(x * x, axis=-1, keepdims=True) + EPS) * g


def _rope_block(x, cos_blk, sin_blk, lane):
    fwd = pltpu.roll(x, HEAD_PAD - ROPE_HALF, axis=1)
    bwd = pltpu.roll(x, ROPE_HALF, axis=1)
    rot = jnp.where(lane < ROPE_MID, fwd, bwd)
    return x * cos_blk + rot * sin_blk


def _inproj_kernel(x_ref, mod_ref, gmix_ref, win_ref, gq_ref, wuq_ref, gkv_ref, wuk_ref, wv_ref,
                   convw_ref, convb_ref, bg_ref, cos_ref, sin_ref,
                   qa_ref, ka_ref, vt_ref, qmt_ref, km_ref, vmt_ref, og_ref, gcol_ref, grow_ref,
                   zbuf_ref):
    tm = x_ref.shape[1]
    si = pl.program_id(1)
    shift = mod_ref[0, 0:1, :]
    gain = gmix_ref[...] * (1.0 + mod_ref[0, 1:2, :])
    lane = _lane_iota((TM_SUB, LANES))

    @pl.when(si == 0)
    def _():
        zbuf_ref[0:SUBLANES, :] = jnp.zeros((SUBLANES, 2 * M_QK), F32)

    for r0 in range(0, tm, TM_SUB):
        rows = slice(r0, r0 + TM_SUB)
        hb = (_rms(x_ref[0, rows, :], gain) + shift).astype(BF16)
        cos_blk = cos_ref[0, rows, :]
        sin_blk = sin_ref[0, rows, :]

        ql = _dot(hb, win_ref[:, C_Q:C_KV])
        qn = _rms(ql, gq_ref[...]).astype(BF16)
        qa = _dot(qn, wuq_ref[...]) * ((MLA_NOPE + MLA_ROPE) ** -0.5 * LOG2E)
        for hd in range(MLA_HEADS):
            blk = qa[:, hd * HEAD_PAD:(hd + 1) * HEAD_PAD]
            qa_ref[0, rows, hd * HEAD_PAD:(hd + 1) * HEAD_PAD] = (
                _rope_block(blk, cos_blk, sin_blk, lane).astype(BF16))

        zs = _dot(hb, win_ref[:, C_S:D_IN_PAD])
        kr = _rope_block(zs, cos_blk, sin_blk, lane)
        kr = jnp.where((lane >= ROPE_LO) & (lane < ROPE_HI), kr, 0.0)
        gpre = zs + bg_ref[...]
        gates = jnp.where(lane < M_HEADS, gpre, _log_sigmoid(gpre))
        gcol_ref[0, rows, :] = gates
        grow_ref[0, :, rows] = jnp.transpose(gates)[0:SUBLANES, :]

        kl = _dot(hb, win_ref[:, C_KV:C_QK])
        kn = _rms(kl, gkv_ref[...]).astype(BF16)
        ka = _dot(kn, wuk_ref[...])
        for hd in range(MLA_HEADS):
            ka_ref[0, rows, hd * HEAD_PAD:(hd + 1) * HEAD_PAD] = (
                ka[:, hd * HEAD_PAD:(hd + 1) * HEAD_PAD] + kr).astype(BF16)
        vt_ref[0, :, rows] = jnp.transpose(_dot(kn, wv_ref[...])).astype(BF16)

        zqk = _dot(hb, win_ref[:, C_QK:C_V])
        z0 = SUBLANES + r0
        zbuf_ref[z0:z0 + TM_SUB, :] = zqk
        acc = zqk * convw_ref[CONV_W - 1:CONV_W, :] + convb_ref[...]
        for j in range(1, CONV_W):
            acc = acc + zbuf_ref[z0 - j:z0 - j + TM_SUB, :] * convw_ref[CONV_W - 1 - j:CONV_W - j, :]
        qk = acc * _sigmoid(acc)
        qmt_ref[0, :, rows] = jnp.transpose(qk[:, 0:M_QK]).astype(BF16)
        km_ref[0, rows, :] = (qk[:, M_QK:2 * M_QK] * (M_DK ** -0.5)).astype(BF16)

        vmt_ref[0, :, rows] = jnp.transpose(_dot(hb, win_ref[:, C_V:C_O])).astype(BF16)
        og_ref[0, rows, :] = _sigmoid(_dot(hb, win_ref[:, C_O:C_S])).astype(BF16)

    zbuf_ref[0:SUBLANES, :] = zbuf_ref[tm:tm + SUBLANES, :]


def _const_spec(shape):
    nd = len(shape)
    return pl.BlockSpec(shape, lambda *_: (0,) * nd, pipeline_mode=pl.Buffered(1))


def _inproj(x, mod, gmix, win_p, gq, wuq_p, gkv, wuk_p, wv_p, convw, convb, bg_blk, cos_blk, sin_blk):
    bsz, s, d = x.shape
    tm = TM_IN
    tok = lambda w: pl.BlockSpec((1, tm, w), lambda b, i: (b, i, 0))
    out_shapes = (
        jax.ShapeDtypeStruct((bsz, s, QK_PAD), BF16),
        jax.ShapeDtypeStruct((bsz, s, QK_PAD), BF16),
        jax.ShapeDtypeStruct((bsz, MLA_OUT, s), BF16),
        jax.ShapeDtypeStruct((bsz, M_QK, s), BF16),
        jax.ShapeDtypeStruct((bsz, s, M_QK), BF16),
        jax.ShapeDtypeStruct((bsz, M_OUT, s), BF16),
        jax.ShapeDtypeStruct((bsz, s, M_OUT), BF16),
        jax.ShapeDtypeStruct((bsz, s, LANES), F32),
        jax.ShapeDtypeStruct((bsz, SUBLANES, s), F32),
    )
    out_specs = (
        tok(QK_PAD), tok(QK_PAD),
        pl.BlockSpec((1, MLA_OUT, tm), lambda b, i: (b, 0, i)),
        pl.BlockSpec((1, M_QK, tm), lambda b, i: (b, 0, i)),
        tok(M_QK),
        pl.BlockSpec((1, M_OUT, tm), lambda b, i: (b, 0, i)),
        tok(M_OUT), tok(LANES),
        pl.BlockSpec((1, SUBLANES, tm), lambda b, i: (b, 0, i)),
    )
    in_specs = [
        tok(d),
        pl.BlockSpec((1, 6, d), lambda b, i: (b, 0, 0)),
        _const_spec((1, d)),
        _const_spec(win_p.shape),
        _const_spec((1, MLA_Q_RANK)),
        _const_spec(wuq_p.shape),
        _const_spec((1, MLA_KV_RANK)),
        _const_spec(wuk_p.shape),
        _const_spec(wv_p.shape),
        _const_spec((CONV_W, 2 * M_QK)),
        _const_spec((1, 2 * M_QK)),
        _const_spec((1, LANES)),
        tok(LANES), tok(LANES),
    ]
    return pl.pallas_call(
        _inproj_kernel,
        out_shape=out_shapes,
        grid=(bsz, s // tm),
        in_specs=in_specs,
        out_specs=out_specs,
        scratch_shapes=[pltpu.VMEM((tm + SUBLANES, 2 * M_QK), F32)],
        compiler_params=pltpu.CompilerParams(
            dimension_semantics=("parallel", "arbitrary"), vmem_limit_bytes=VMEM_LIMIT),
        name="inproj",
    )(x, mod, gmix, win_p, gq, wuq_p, gkv, wuk_p, wv_p, convw, convb, bg_blk, cos_blk, sin_blk)


def _mla_kernel(q_ref, k_ref, vt_ref, g_ref, o_ref, m_ref, acc_ref):
    s = q_ref.shape[1]
    half = TQ // 2
    causal_a = _row_iota((half, TQ)) <= _lane_iota((half, TQ))
    causal_b = _row_iota((half, half)) <= _lane_iota((half, half))
    ones_rows = jnp.ones((ONES_ROWS, TQ), BF16)

    def q_group(a, carry):
        q0s = [pl.multiple_of((MLA_QG * a + t) * TQ, TQ) for t in range(MLA_QG)]
        qs = [[q_ref[0, pl.ds(q0s[t], TQ), h * HEAD_PAD:(h + 1) * HEAD_PAD] for h in range(MLA_HPS)]
              for t in range(MLA_QG)]
        for c in range(MLA_QG * MLA_HPS):
            m_ref[c] = jnp.full((1, TQ), -jnp.inf, F32)
            acc_ref[c] = jnp.zeros((MLA_V + ONES_ROWS, TQ), F32)

        def kv_step(k0, tiles):
            chains = [(t, h) for t in tiles for h in range(MLA_HPS)]
            ks = [k_ref[0, pl.ds(k0, TQ), h * HEAD_PAD:(h + 1) * HEAD_PAD] for h in range(MLA_HPS)]
            sts = [_dot_nt(ks[h], qs[t][h]) for t, h in chains]
            upd = []
            for (t, h), st in zip(chains, sts):
                c = t * MLA_HPS + h
                m = m_ref[c]
                m_new = jnp.maximum(m, jnp.max(st, axis=0, keepdims=True))
                p = jnp.exp2(st - m_new).astype(BF16)
                vt = jnp.concatenate([vt_ref[0, h * MLA_V:(h + 1) * MLA_V, pl.ds(k0, TQ)], ones_rows], axis=0)
                upd.append((c, m_new, jnp.exp2(m - m_new), _dot(vt, p)))
            for c, m_new, alpha, pv in upd:
                m_ref[c] = m_new
                acc_ref[c] = alpha * acc_ref[c] + pv

        def diag_step(t):
            k0 = q0s[t]
            k1 = pl.multiple_of(k0 + half, half)
            heads = range(MLA_HPS)
            sa = [_dot_nt(k_ref[0, pl.ds(k0, half), h * HEAD_PAD:(h + 1) * HEAD_PAD], qs[t][h]) for h in heads]
            sb = [_dot_nt(k_ref[0, pl.ds(k1, half), h * HEAD_PAD:(h + 1) * HEAD_PAD], qs[t][h][half:, :])
                  for h in heads]
            upd = []
            for h in heads:
                c = t * MLA_HPS + h
                st_a = jnp.where(causal_a, sa[h], -jnp.inf)
                st_b = jnp.where(causal_b, sb[h], -jnp.inf)
                m = m_ref[c]
                m_a = jnp.maximum(m, jnp.max(st_a, axis=0, keepdims=True))
                m_b = jnp.maximum(m_a[:, half:], jnp.max(st_b, axis=0, keepdims=True))
                m_new = jnp.concatenate([m_a[:, :half], m_b], axis=1)
                p_a = jnp.exp2(st_a - m_new).astype(BF16)
                p_b = jnp.exp2(st_b - m_b).astype(BF16)
                rows = slice(h * MLA_V, (h + 1) * MLA_V)
                vt_a = jnp.concatenate([vt_ref[0, rows, pl.ds(k0, half)], ones_rows[:, :half]], axis=0)
                vt_b = jnp.concatenate([vt_ref[0, rows, pl.ds(k1, half)], ones_rows[:, :half]], axis=0)
                upd.append((c, m_new, jnp.exp2(m - m_new), _dot(vt_a, p_a), _dot(vt_b, p_b)))
            for c, m_new, alpha, pv_a, pv_b in upd:
                acc = alpha * acc_ref[c] + pv_a
                m_ref[c] = m_new
                acc_ref[c] = jnp.concatenate([acc[:, :half], acc[:, half:] + pv_b], axis=1)

        all_tiles = tuple(range(MLA_QG))

        def body(kj, carry2):
            kv_step(pl.multiple_of(kj * TQ, TQ), all_tiles)
            return carry2

        lax.fori_loop(0, MLA_QG * a, body, 0)
        for t in range(MLA_QG):
            diag_step(t)
            if t + 1 < MLA_QG:
                kv_step(q0s[t], all_tiles[t + 1:])

        for t in range(MLA_QG):
            for pr in range(MLA_HPS // 2):
                halves = []
                for h in (2 * pr, 2 * pr + 1):
                    acc = acc_ref[t * MLA_HPS + h]
                    o = acc[0:MLA_V] * (1.0 / acc[MLA_V:MLA_V + 1])
                    ms = jnp.mean(o * o, axis=0, keepdims=True)
                    halves.append(o * lax.rsqrt(ms + EPS))
                y = jnp.transpose(jnp.concatenate(halves, axis=0))
                y = y * g_ref[:, pr * LANES:(pr + 1) * LANES]
                o_ref[0, pl.ds(q0s[t], TQ), pr * LANES:(pr + 1) * LANES] = y.astype(o_ref.dtype)
        return carry

    lax.fori_loop(0, s // (MLA_QG * TQ), q_group, 0)


def _mla(qa, ka, vt, g_out):
    bsz, s, _ = qa.shape
    groups = MLA_HEADS // MLA_HPS
    return pl.pallas_call(
        _mla_kernel,
        out_shape=jax.ShapeDtypeStruct((bsz, s, MLA_OUT), BF16),
        grid=(bsz, groups),
        in_specs=[
            pl.BlockSpec((1, s, MLA_HPS * HEAD_PAD), lambda b, p: (b, 0, p)),
            pl.BlockSpec((1, s, MLA_HPS * HEAD_PAD), lambda b, p: (b, 0, p)),
            pl.BlockSpec((1, MLA_HPS * MLA_V, s), lambda b, p: (b, p, 0)),
            pl.BlockSpec((1, MLA_HPS * MLA_V), lambda b, p: (0, p)),
        ],
        out_specs=pl.BlockSpec((1, s, MLA_HPS * MLA_V), lambda b, p: (b, 0, p)),
        scratch_shapes=[pltpu.VMEM((MLA_QG * MLA_HPS, 1, TQ), F32),
                        pltpu.VMEM((MLA_QG * MLA_HPS, MLA_V + ONES_ROWS, TQ), F32)],
        compiler_params=pltpu.CompilerParams(
            dimension_semantics=("parallel", "parallel"), vmem_limit_bytes=VMEM_LIMIT),
        name="mla_attention",
    )(qa, ka, vt, g_out)


def _split3(x):
    hi = x.astype(BF16)
    r1 = x - hi.astype(F32)
    mid = r1.astype(BF16)
    lo = (r1 - mid.astype(F32)).astype(BF16)
    return hi, mid, lo


def _mlstm_kernel(qt_ref, k_ref, vt_ref, og_ref, gcol_ref, grow_ref, g_ref, o_ref, cn_ref, m_ref):
    L = k_ref.shape[1]
    ci = pl.program_id(1)

    @pl.when(ci == 0)
    def _():
        cn_ref[...] = jnp.zeros(cn_ref.shape, F32)
        m_ref[...] = jnp.zeros(m_ref.shape, F32)

    row = _row_iota((L, L))
    col = _lane_iota((L, L))
    upper = row <= col
    tri = jnp.where(row >= col, 1.0, 0.0).astype(BF16)
    tri_t = jnp.where(upper, 1.0, 0.0).astype(BF16)

    lane1 = _lane_iota((1, LANES))
    head_lane = (lane1 >= M_HEADS) & (lane1 < 2 * M_HEADS)
    lane128 = _lane_iota((L, LANES))
    ones_rows = jnp.ones((ONES_ROWS, L), BF16)
    chains = [(bi, h) for bi in range(k_ref.shape[0]) for h in range(M_HEADS)]

    gate_stats = []
    for bi in range(k_ref.shape[0]):
        gcol = gcol_ref[bi]
        grow = grow_ref[bi]
        ccol = sum(_dot(tri, part) for part in _split3(gcol))
        crow = sum(_dot(part, tri_t) for part in _split3(grow))
        c_all = ccol - pltpu.roll(gcol, M_HEADS, axis=1)
        btot_all = ccol[L - 1:L, :]
        m0_all = m_ref[bi]
        g_all = btot_all - c_all
        m_loc_all = jnp.max(g_all, axis=0, keepdims=True)
        wgt_all = jnp.exp(g_all - m_loc_all)
        m_new_all = jnp.where(head_lane, jnp.maximum(btot_all + m0_all, m_loc_all), 0.0)
        a_all = jnp.exp(btot_all + m0_all - m_new_all)
        e_all = jnp.exp(m_loc_all - m_new_all)
        m_ref[bi] = m_new_all
        gate_stats.append((crow, c_all, m0_all, wgt_all, a_all, e_all))

    def head_keys(bi, h):
        pr, half = divmod(h, 2)
        kp = k_ref[bi, :, pr * LANES:(pr + 1) * LANES]
        in_half = (lane128 >= half * M_DK) & (lane128 < (half + 1) * M_DK)
        return kp, in_half

    sts = []
    for bi, h in chains:
        kp, in_half = head_keys(bi, h)
        kh = jnp.where(in_half, kp, jnp.zeros_like(kp))
        sts.append(_dot(kh, qt_ref[bi, (h // 2) * LANES:(h // 2 + 1) * LANES, :]))

    for (bi, h), st in zip(chains, sts):
        crow, c_all, m0_all, wgt_all, a_all, e_all = gate_stats[bi]
        hl = M_HEADS + h
        bc_row = crow[hl:hl + 1, :]
        m0 = m0_all[:, hl:hl + 1]
        log_d = jnp.where(upper, bc_row - c_all[:, hl:hl + 1], -jnp.inf)
        m_d = jnp.max(log_d, axis=0, keepdims=True)
        log_inter = bc_row + m0
        m_t = jnp.maximum(log_inter, m_d)
        d_m = jnp.exp(log_d - m_t)
        inter = jnp.exp(log_inter - m_t)
        sc = (st * d_m).astype(BF16)
        vaug = jnp.concatenate([vt_ref[bi, h * M_DV:(h + 1) * M_DV, :], ones_rows], axis=0)
        intra = _dot(vaug, sc)
        cn = cn_ref[bi * M_HEADS + h]
        cross = _dot(cn.astype(BF16), qt_ref[bi, (h // 2) * LANES:(h // 2 + 1) * LANES, :])
        num = intra[0:M_DV] + inter * cross[0:M_DV]
        den = intra[M_DV:M_DV + 1] + inter * cross[M_DV:M_DV + 1]
        hh = jnp.transpose(num * (1.0 / jnp.maximum(jnp.abs(den), jnp.exp(-m_t))))
        y = og_ref[bi, :, h * M_DV:(h + 1) * M_DV].astype(F32) * hh
        y = _rms(y, g_ref[:, h * M_DV:(h + 1) * M_DV])
        o_ref[bi, :, h * M_DV:(h + 1) * M_DV] = y.astype(o_ref.dtype)

        kp, in_half = head_keys(bi, h)
        kw = (jnp.where(in_half, kp.astype(F32), 0.0) * wgt_all[:, hl:hl + 1]).astype(BF16)
        d_cn = _dot(vaug, kw)
        cn_ref[bi * M_HEADS + h] = a_all[:, hl:hl + 1] * cn + e_all[:, hl:hl + 1] * d_cn


def _mlstm(qmt, km, vmt, og, gcol, grow, g_out):
    bsz, s, _ = km.shape
    L = M_CHUNK
    nb = M_SEQS
    tok = lambda w: pl.BlockSpec((nb, L, w), lambda b, c: (b, c, 0))
    tok_t = lambda w: pl.BlockSpec((nb, w, L), lambda b, c: (b, 0, c))
    return pl.pallas_call(
        _mlstm_kernel,
        out_shape=jax.ShapeDtypeStruct((bsz, s, M_OUT), BF16),
        grid=(bsz // nb, s // L),
        in_specs=[
            tok_t(M_QK), tok(M_QK), tok_t(M_OUT), tok(M_OUT), tok(LANES), tok_t(SUBLANES),
            pl.BlockSpec((1, M_OUT), lambda b, c: (0, 0)),
        ],
        out_specs=tok(M_OUT),
        scratch_shapes=[pltpu.VMEM((nb * M_HEADS, M_DV + ONES_ROWS, LANES), F32),
                        pltpu.VMEM((nb, 1, LANES), F32)],
        compiler_params=pltpu.CompilerParams(
            dimension_semantics=("parallel", "arbitrary"), vmem_limit_bytes=VMEM_LIMIT),
        name="mlstm",
    )(qmt, km, vmt, og, gcol, grow, g_out)


def _outffn_kernel(x_ref, ya_ref, yb_ref, mod_ref, wout_ref, gffn_ref, wg_ref, wu_ref, wd_ref, gfin_ref,
                   o_ref, *, final_norm):
    gate_a = mod_ref[0, 2:3, :]
    shift_f = mod_ref[0, 3:4, :]
    gain_f = gffn_ref[...] * (1.0 + mod_ref[0, 4:5, :])
    gate_f = mod_ref[0, 5:6, :]
    d_ff = wg_ref.shape[1]
    step = -(-d_ff // FF_CHUNKS // (2 * LANES)) * (2 * LANES)

    for r0 in range(0, x_ref.shape[1], TM_OUT_SUB):
        rows = slice(r0, r0 + TM_OUT_SUB)
        mix = (_dot(ya_ref[0, rows, :], wout_ref[0:MLA_OUT, :])
               + _dot(yb_ref[0, rows, :], wout_ref[MLA_OUT:MLA_OUT + M_OUT, :]))
        x1 = x_ref[0, rows, :] + gate_a * mix
        hb = (_rms(x1, gain_f) + shift_f).astype(BF16)
        ffn = None
        for c0 in range(0, d_ff, step):
            c1 = min(c0 + step, d_ff)
            gt = _dot(hb, wg_ref[:, c0:c1])
            up = _dot(hb, wu_ref[:, c0:c1])
            act = (gt * _sigmoid(gt) * up).astype(BF16)
            part = _dot(act, wd_ref[c0:c1, :])
            ffn = part if ffn is None else ffn + part
        x2 = x1 + gate_f * ffn
        if final_norm:
            x2 = _rms(x2, gfin_ref[...])
        o_ref[0, rows, :] = x2


def _outffn(x, ya, yb, mod, wout, gffn, wg, wu, wd, gfin, final_norm):
    bsz, s, d = x.shape
    tm = TM_OUT
    tok = lambda w: pl.BlockSpec((1, tm, w), lambda b, i: (b, i, 0))
    return pl.pallas_call(
        functools.partial(_outffn_kernel, final_norm=final_norm),
        out_shape=jax.ShapeDtypeStruct((bsz, s, d), F32),
        grid=(bsz, s // tm),
        in_specs=[
            tok(d), tok(MLA_OUT), tok(M_OUT),
            pl.BlockSpec((1, 6, d), lambda b, i: (b, 0, 0)),
            _const_spec(wout.shape),
            _const_spec((1, d)),
            _const_spec(wg.shape), _const_spec(wu.shape), _const_spec(wd.shape),
            _const_spec((1, d)),
        ],
        out_specs=tok(d),
        compiler_params=pltpu.CompilerParams(
            dimension_semantics=("parallel", "parallel"), vmem_limit_bytes=VMEM_LIMIT),
        name="outproj_ffn",
    )(x, ya, yb, mod, wout, gffn, wg, wu, wd, gfin)


def _prep_w_in(w_in):
    d = w_in.shape[0]
    o_q, o_kv, o_kr = 0, MLA_Q_RANK, MLA_Q_RANK + MLA_KV_RANK
    o_qk = o_kr + MLA_ROPE
    o_v = o_qk + 2 * M_QK
    o_o = o_v + M_OUT
    o_i = o_o + M_OUT
    o_f = o_i + M_HEADS
    small = jnp.concatenate([
        w_in[:, o_i:o_i + M_HEADS], w_in[:, o_f:o_f + M_HEADS],
        jnp.zeros((d, ROPE_LO - 2 * M_HEADS), w_in.dtype),
        w_in[:, o_kr:o_kr + MLA_ROPE],
        jnp.zeros((d, HEAD_PAD - ROPE_HI), w_in.dtype)], axis=1)
    return jnp.concatenate([w_in[:, o_q:o_kr], w_in[:, o_qk:o_i], small], axis=1).astype(BF16)


def _prep_w_uq(w_uq):
    r = w_uq.shape[0]
    w = w_uq.reshape(r, MLA_HEADS, MLA_NOPE + MLA_ROPE)
    w = jnp.pad(w, ((0, 0), (0, 0), (0, HEAD_PAD - MLA_NOPE - MLA_ROPE)))
    return w.reshape(r, QK_PAD).astype(BF16)


def _prep_w_ukv(w_ukv):
    r = w_ukv.shape[0]
    w = w_ukv.reshape(r, MLA_HEADS, MLA_NOPE + MLA_V)
    wk = jnp.pad(w[:, :, :MLA_NOPE], ((0, 0), (0, 0), (0, HEAD_PAD - MLA_NOPE))).reshape(r, QK_PAD)
    wv = w[:, :, MLA_NOPE:].reshape(r, MLA_OUT)
    return wk.astype(BF16), wv.astype(BF16)


def kernel(x, c, positions, w_ada, b_ada, g_mix, w_in, g_q, w_uq, g_kv, w_ukv, conv_w, conv_b, b_gates,
           g_out_mla, g_out_mlstm, w_out, g_ffn, w_gate, w_up, w_down, g_final):
    bsz, s, d = x.shape
    depth = w_ada.shape[0]
    cos_blk, sin_blk = _rope_tables(positions)
    row = lambda v: v.reshape(1, -1)
    for l in range(depth):
        mod = _adaln(c, w_ada[l], b_ada[l]).reshape(bsz, 6, d)
        wuk_p, wv_p = _prep_w_ukv(w_ukv[l])
        bg_blk = jnp.pad(b_gates[l], (0, LANES - 2 * M_HEADS)).reshape(1, LANES)
        qa, ka, vt, qmt, km, vmt, og, gcol, grow = _inproj(
            x, mod, row(g_mix[l]), _prep_w_in(w_in[l]), row(g_q[l]), _prep_w_uq(w_uq[l]),
            row(g_kv[l]), wuk_p, wv_p, conv_w[l], row(conv_b[l]), bg_blk, cos_blk, sin_blk)
        ya = _mla(qa, ka, vt, row(g_out_mla[l]))
        yb = _mlstm(qmt, km, vmt, og, gcol, grow, row(g_out_mlstm[l]))
        x = _outffn(x, ya, yb, mod, w_out[l].astype(BF16), row(g_ffn[l]),
                    w_gate[l].astype(BF16), w_up[l].astype(BF16), w_down[l].astype(BF16),
                    row(g_final), final_norm=(l == depth - 1))
    return x
```

```python
import functools

import numpy as np
import jax
import jax.numpy as jnp
from jax import lax
from jax.experimental import pallas as pl
from jax.experimental.pallas import tpu as pltpu

F32 = jnp.float32
BF16 = jnp.bfloat16

LANES = 128
SUBLANES = 8

MLA_HEADS = 8
MLA_NOPE = 64
MLA_ROPE = 32
MLA_V = 64
MLA_Q_RANK = 384
MLA_KV_RANK = 256
ROPE_THETA = 10000.0
ROPE_HALF = MLA_ROPE // 2
M_HEADS = 4
M_DK = 64
M_DV = 128
CONV_W = 4
EPS = 1e-6
LOG2E = 1.4426950408889634

MLA_OUT = MLA_HEADS * MLA_V
M_OUT = M_HEADS * M_DV
M_QK = M_HEADS * M_DK
HEAD_PAD = LANES
QK_PAD = MLA_HEADS * HEAD_PAD
ROPE_LO = MLA_NOPE
ROPE_MID = MLA_NOPE + ROPE_HALF
ROPE_HI = MLA_NOPE + MLA_ROPE

C_Q = 0
C_KV = C_Q + MLA_Q_RANK
C_QK = C_KV + MLA_KV_RANK
C_V = C_QK + 2 * M_QK
C_O = C_V + M_OUT
C_S = C_O + M_OUT
D_IN_PAD = C_S + LANES

TM_ROPE = 1024
TM_IN = 1024
TM_SUB = 512
TQ = 512
MLA_HPS = 4
MLA_QG = 2
ONES_ROWS = 16
M_CHUNK = 256
M_SEQS = 1
TM_OUT = 1024
TM_OUT_SUB = 512
FF_CHUNKS = 3

VMEM_LIMIT = 56 * 1024 * 1024


def _lane_iota(shape):
    return lax.broadcasted_iota(jnp.int32, shape, len(shape) - 1)


def _row_iota(shape):
    return lax.broadcasted_iota(jnp.int32, shape, len(shape) - 2)


def _dot(a, b):
    return jnp.dot(a, b, preferred_element_type=F32)


def _dot_nt(a, b):
    return lax.dot_general(a, b, (((1,), (1,)), ((), ())), preferred_element_type=F32)


def _log_sigmoid(x):
    return jnp.minimum(x, 0.0) - jnp.log1p(jnp.exp(-jnp.abs(x)))


def _sigmoid(x):
    return 1.0 / (1.0 + jnp.exp(-x))


def _adaln_kernel(c_ref, w_ref, b_ref, o_ref):
    c = c_ref[...]
    cond = c * _sigmoid(c)
    o_ref[...] = _dot(cond, w_ref[...]) + b_ref[...]


def _adaln(c, w, b):
    bsz, d = c.shape
    n = w.shape[1]
    tn = 1536
    return pl.pallas_call(
        _adaln_kernel,
        out_shape=jax.ShapeDtypeStruct((bsz, n), F32),
        grid=(n // tn,),
        in_specs=[
            pl.BlockSpec((bsz, d), lambda j: (0, 0)),
            pl.BlockSpec((d, tn), lambda j: (0, j)),
            pl.BlockSpec((1, tn), lambda j: (0, j)),
        ],
        out_specs=pl.BlockSpec((bsz, tn), lambda j: (0, j)),
        compiler_params=pltpu.CompilerParams(
            dimension_semantics=("parallel",), vmem_limit_bytes=VMEM_LIMIT),
        name="adaln",
    )(c, w, b.reshape(1, n))


def _rope_kernel(pos_ref, inv_ref, cos_ref, sin_ref):
    ang = inv_ref[...] * pos_ref[0].astype(F32)
    c = jnp.cos(ang)
    sn = jnp.sin(ang)
    t = ang.shape[1]
    lo = jnp.ones((ROPE_LO, t), F32)
    hi = jnp.ones((HEAD_PAD - ROPE_HI, t), F32)
    cos_ref[0] = jnp.transpose(jnp.concatenate([lo, c, c, hi], axis=0))
    sin_ref[0] = jnp.transpose(jnp.concatenate([0.0 * lo, -sn, sn, 0.0 * hi], axis=0))


def _rope_tables(positions):
    bsz, s = positions.shape
    t = TM_ROPE
    inv = ROPE_THETA ** (-np.arange(ROPE_HALF, dtype=np.float64) / ROPE_HALF)
    inv = jnp.asarray(inv.astype(np.float32).reshape(ROPE_HALF, 1))
    out = jax.ShapeDtypeStruct((bsz, s, HEAD_PAD), F32)
    blk = pl.BlockSpec((1, t, HEAD_PAD), lambda b, i: (b, i, 0))
    return pl.pallas_call(
        _rope_kernel,
        out_shape=(out, out),
        grid=(bsz, s // t),
        in_specs=[
            pl.BlockSpec((1, 1, t), lambda b, i: (b, 0, i)),
            pl.BlockSpec((ROPE_HALF, 1), lambda b, i: (0, 0)),
        ],
        out_specs=(blk, blk),
        compiler_params=pltpu.CompilerParams(
            dimension_semantics=("parallel", "parallel"), vmem_limit_bytes=VMEM_LIMIT),
        name="rope_tables",
    )(positions.reshape(bsz, 1, s), inv)


def _rms(x, g):
    return x * lax.rsqrt(jnp.mean(x * x, axis=-1, keepdims=True) + EPS) * g


def _rope_block(x, cos_blk, sin_blk, lane):
    fwd = pltpu.roll(x, HEAD_PAD - ROPE_HALF, axis=1)
    bwd = pltpu.roll(x, ROPE_HALF, axis=1)
    rot = jnp.where(lane < ROPE_MID, fwd, bwd)
    return x * cos_blk + rot * sin_blk


def _inproj_kernel(x_ref, mod_ref, gmix_ref, win_ref, gq_ref, wuq_ref, gkv_ref, wuk_ref, wv_ref,
                   convw_ref, convb_ref, bg_ref, cos_ref, sin_ref,
                   qa_ref, ka_ref, vt_ref, qmt_ref, km_ref, vmt_ref, og_ref, gcol_ref, grow_ref,
                   zbuf_ref):
    tm = x_ref.shape[1]
    si = pl.program_id(1)
    shift = mod_ref[0, 0:1, :]
    gain = gmix_ref[...] * (1.0 + mod_ref[0, 1:2, :])
    lane = _lane_iota((TM_SUB, LANES))

    @pl.when(si == 0)
    def _():
        zbuf_ref[0:SUBLANES, :] = jnp.zeros((SUBLANES, 2 * M_QK), F32)

    for r0 in range(0, tm, TM_SUB):
        rows = slice(r0, r0 + TM_SUB)
        hb = (_rms(x_ref[0, rows, :], gain) + shift).astype(BF16)
        cos_blk = cos_ref[0, rows, :]
        sin_blk = sin_ref[0, rows, :]

        ql = _dot(hb, win_ref[:, C_Q:C_KV])
        qn = _rms(ql, gq_ref[...]).astype(BF16)
        qa = _dot(qn, wuq_ref[...]) * ((MLA_NOPE + MLA_ROPE) ** -0.5 * LOG2E)
        for hd in range(MLA_HEADS):
            blk = qa[:, hd * HEAD_PAD:(hd + 1) * HEAD_PAD]
            qa_ref[0, rows, hd * HEAD_PAD:(hd + 1) * HEAD_PAD] = (
                _rope_block(blk, cos_blk, sin_blk, lane).astype(BF16))

        zs = _dot(hb, win_ref[:, C_S:D_IN_PAD])
        kr = _rope_block(zs, cos_blk, sin_blk, lane)
        kr = jnp.where((lane >= ROPE_LO) & (lane < ROPE_HI), kr, 0.0)
        gpre = zs + bg_ref[...]
        gates = jnp.where(lane < M_HEADS, gpre, _log_sigmoid(gpre))
        gcol_ref[0, rows, :] = gates
        grow_ref[0, :, rows] = jnp.transpose(gates)[0:SUBLANES, :]

        kl = _dot(hb, win_ref[:, C_KV:C_QK])
        kn = _rms(kl, gkv_ref[...]).astype(BF16)
        ka = _dot(kn, wuk_ref[...])
        for hd in range(MLA_HEADS):
            ka_ref[0, rows, hd * HEAD_PAD:(hd + 1) * HEAD_PAD] = (
                ka[:, hd * HEAD_PAD:(hd + 1) * HEAD_PAD] + kr).astype(BF16)
        vt_ref[0, :, rows] = jnp.transpose(_dot(kn, wv_ref[...])).astype(BF16)

        zqk = _dot(hb, win_ref[:, C_QK:C_V])
        z0 = SUBLANES + r0
        zbuf_ref[z0:z0 + TM_SUB, :] = zqk
        acc = zqk * convw_ref[CONV_W - 1:CONV_W, :] + convb_ref[...]
        for j in range(1, CONV_W):
            acc = acc + zbuf_ref[z0 - j:z0 - j + TM_SUB, :] * convw_ref[CONV_W - 1 - j:CONV_W - j, :]
        qk = acc * _sigmoid(acc)
        qmt_ref[0, :, rows] = jnp.transpose(qk[:, 0:M_QK]).astype(BF16)
        km_ref[0, rows, :] = (qk[:, M_QK:2 * M_QK] * (M_DK ** -0.5)).astype(BF16)

        vmt_ref[0, :, rows] = jnp.transpose(_dot(hb, win_ref[:, C_V:C_O])).astype(BF16)
        og_ref[0, rows, :] = _sigmoid(_dot(hb, win_ref[:, C_O:C_S])).astype(BF16)

    zbuf_ref[0:SUBLANES, :] = zbuf_ref[tm:tm + SUBLANES, :]


def _const_spec(shape):
    nd = len(shape)
    return pl.BlockSpec(shape, lambda *_: (0,) * nd, pipeline_mode=pl.Buffered(1))


def _inproj(x, mod, gmix, win_p, gq, wuq_p, gkv, wuk_p, wv_p, convw, convb, bg_blk, cos_blk, sin_blk):
    bsz, s, d = x.shape
    tm = TM_IN
    tok = lambda w: pl.BlockSpec((1, tm, w), lambda b, i: (b, i, 0))
    out_shapes = (
        jax.ShapeDtypeStruct((bsz, s, QK_PAD), BF16),
        jax.ShapeDtypeStruct((bsz, s, QK_PAD), BF16),
        jax.ShapeDtypeStruct((bsz, MLA_OUT, s), BF16),
        jax.ShapeDtypeStruct((bsz, M_QK, s), BF16),
        jax.ShapeDtypeStruct((bsz, s, M_QK), BF16),
        jax.ShapeDtypeStruct((bsz, M_OUT, s), BF16),
        jax.ShapeDtypeStruct((bsz, s, M_OUT), BF16),
        jax.ShapeDtypeStruct((bsz, s, LANES), F32),
        jax.ShapeDtypeStruct((bsz, SUBLANES, s), F32),
    )
    out_specs = (
        tok(QK_PAD), tok(QK_PAD),
        pl.BlockSpec((1, MLA_OUT, tm), lambda b, i: (b, 0, i)),
        pl.BlockSpec((1, M_QK, tm), lambda b, i: (b, 0, i)),
        tok(M_QK),
        pl.BlockSpec((1, M_OUT, tm), lambda b, i: (b, 0, i)),
        tok(M_OUT), tok(LANES),
        pl.BlockSpec((1, SUBLANES, tm), lambda b, i: (b, 0, i)),
    )
    in_specs = [
        tok(d),
        pl.BlockSpec((1, 6, d), lambda b, i: (b, 0, 0)),
        _const_spec((1, d)),
        _const_spec(win_p.shape),
        _const_spec((1, MLA_Q_RANK)),
        _const_spec(wuq_p.shape),
        _const_spec((1, MLA_KV_RANK)),
        _const_spec(wuk_p.shape),
        _const_spec(wv_p.shape),
        _const_spec((CONV_W, 2 * M_QK)),
        _const_spec((1, 2 * M_QK)),
        _const_spec((1, LANES)),
        tok(LANES), tok(LANES),
    ]
    return pl.pallas_call(
        _inproj_kernel,
        out_shape=out_shapes,
        grid=(bsz, s // tm),
        in_specs=in_specs,
        out_specs=out_specs,
        scratch_shapes=[pltpu.VMEM((tm + SUBLANES, 2 * M_QK), F32)],
        compiler_params=pltpu.CompilerParams(
            dimension_semantics=("parallel", "arbitrary"), vmem_limit_bytes=VMEM_LIMIT),
        name="inproj",
    )(x, mod, gmix, win_p, gq, wuq_p, gkv, wuk_p, wv_p, convw, convb, bg_blk, cos_blk, sin_blk)


def _mla_kernel(q_ref, k_ref, vt_ref, g_ref, o_ref, m_ref, acc_ref):
    s = q_ref.shape[1]
    half = TQ // 2
    causal_a = _row_iota((half, TQ)) <= _lane_iota((half, TQ))
    causal_b = _row_iota((half, half)) <= _lane_iota((half, half))
    ones_rows = jnp.ones((ONES_ROWS, TQ), BF16)

    def q_group(a, carry):
        q0s = [pl.multiple_of((MLA_QG * a + t) * TQ, TQ) for t in range(MLA_QG)]
        qs = [[q_ref[0, pl.ds(q0s[t], TQ), h * HEAD_PAD:(h + 1) * HEAD_PAD] for h in range(MLA_HPS)]
              for t in range(MLA_QG)]
        for c in range(MLA_QG * MLA_HPS):
            m_ref[c] = jnp.full((1, TQ), -jnp.inf, F32)
            acc_ref[c] = jnp.zeros((MLA_V + ONES_ROWS, TQ), F32)

        def kv_step(k0, full_tiles, diag_tile=None):
            k1 = pl.multiple_of(k0 + half, half)
            work = [(t, h, False) for t in full_tiles for h in range(MLA_HPS)]
            if diag_tile is not None:
                work = [(diag_tile, h, True) for h in range(MLA_HPS)] + work
            scores = []
            for t, h, diag in work:
                lanes = slice(h * HEAD_PAD, (h + 1) * HEAD_PAD)
                if diag:
                    scores.append((_dot_nt(k_ref[0, pl.ds(k0, half), lanes], qs[t][h]),
                                   _dot_nt(k_ref[0, pl.ds(k1, half), lanes], qs[t][h][half:, :])))
                else:
                    scores.append((_dot_nt(k_ref[0, pl.ds(k0, TQ), lanes], qs[t][h]),))
            upd = []
            for (t, h, diag), sc in zip(work, scores):
                c = t * MLA_HPS + h
                rows = slice(h * MLA_V, (h + 1) * MLA_V)
                m = m_ref[c]
                if diag:
                    st_a = jnp.where(causal_a, sc[0], -jnp.inf)
                    st_b = jnp.where(causal_b, sc[1], -jnp.inf)
                    m_a = jnp.maximum(m, jnp.max(st_a, axis=0, keepdims=True))
                    m_b = jnp.maximum(m_a[:, half:], jnp.max(st_b, axis=0, keepdims=True))
                    m_new = jnp.concatenate([m_a[:, :half], m_b], axis=1)
                    p_a = jnp.exp2(st_a - m_new).astype(BF16)
                    p_b = jnp.exp2(st_b - m_b).astype(BF16)
                    vt_a = jnp.concatenate([vt_ref[0, rows, pl.ds(k0, half)], ones_rows[:, :half]], axis=0)
                    vt_b = jnp.concatenate([vt_ref[0, rows, pl.ds(k1, half)], ones_rows[:, :half]], axis=0)
                    upd.append((c, m_new, jnp.exp2(m - m_new), _dot(vt_a, p_a), _dot(vt_b, p_b)))
                else:
                    m_new = jnp.maximum(m, jnp.max(sc[0], axis=0, keepdims=True))
                    p = jnp.exp2(sc[0] - m_new).astype(BF16)
                    vt = jnp.concatenate([vt_ref[0, rows, pl.ds(k0, TQ)], ones_rows], axis=0)
                    upd.append((c, m_new, jnp.exp2(m - m_new), _dot(vt, p), None))
            for c, m_new, alpha, pv, pv_b in upd:
                acc = alpha * acc_ref[c] + pv
                if pv_b is not None:
                    acc = jnp.concatenate([acc[:, :half], acc[:, half:] + pv_b], axis=1)
                m_ref[c] = m_new
                acc_ref[c] = acc

        all_tiles = tuple(range(MLA_QG))

        def body(kj, carry2):
            kv_step(pl.multiple_of(kj * TQ, TQ), all_tiles)
            return carry2

        lax.fori_loop(0, MLA_QG * a, body, 0)
        for t in range(MLA_QG):
            kv_step(q0s[t], all_tiles[t + 1:], diag_tile=t)

        for t in range(MLA_QG):
            for pr in range(MLA_HPS // 2):
                halves = []
                for h in (2 * pr, 2 * pr + 1):
                    acc = acc_ref[t * MLA_HPS + h]
                    o = acc[0:MLA_V] * (1.0 / acc[MLA_V:MLA_V + 1])
                    ms = jnp.mean(o * o, axis=0, keepdims=True)
                    halves.append(o * lax.rsqrt(ms + EPS))
                y = jnp.transpose(jnp.concatenate(halves, axis=0))
                y = y * g_ref[:, pr * LANES:(pr + 1) * LANES]
                o_ref[0, pl.ds(q0s[t], TQ), pr * LANES:(pr + 1) * LANES] = y.astype(o_ref.dtype)
        return carry

    lax.fori_loop(0, s // (MLA_QG * TQ), q_group, 0)


def _mla(qa, ka, vt, g_out):
    bsz, s, _ = qa.shape
    groups = MLA_HEADS // MLA_HPS
    return pl.pallas_call(
        _mla_kernel,
        out_shape=jax.ShapeDtypeStruct((bsz, s, MLA_OUT), BF16),
        grid=(bsz, groups),
        in_specs=[
            pl.BlockSpec((1, s, MLA_HPS * HEAD_PAD), lambda b, p: (b, 0, p)),
            pl.BlockSpec((1, s, MLA_HPS * HEAD_PAD), lambda b, p: (b, 0, p)),
            pl.BlockSpec((1, MLA_HPS * MLA_V, s), lambda b, p: (b, p, 0)),
            pl.BlockSpec((1, MLA_HPS * MLA_V), lambda b, p: (0, p)),
        ],
        out_specs=pl.BlockSpec((1, s, MLA_HPS * MLA_V), lambda b, p: (b, 0, p)),
        scratch_shapes=[pltpu.VMEM((MLA_QG * MLA_HPS, 1, TQ), F32),
                        pltpu.VMEM((MLA_QG * MLA_HPS, MLA_V + ONES_ROWS, TQ), F32)],
        compiler_params=pltpu.CompilerParams(
            dimension_semantics=("parallel", "parallel"), vmem_limit_bytes=VMEM_LIMIT),
        name="mla_attention",
    )(qa, ka, vt, g_out)


def _split3(x):
    hi = x.astype(BF16)
    r1 = x - hi.astype(F32)
    mid = r1.astype(BF16)
    lo = (r1 - mid.astype(F32)).astype(BF16)
    return hi, mid, lo


def _mlstm_kernel(qt_ref, k_ref, vt_ref, og_ref, gcol_ref, grow_ref, g_ref, o_ref, cn_ref, m_ref):
    L = k_ref.shape[1]
    ci = pl.program_id(1)

    @pl.when(ci == 0)
    def _():
        cn_ref[...] = jnp.zeros(cn_ref.shape, F32)
        m_ref[...] = jnp.zeros(m_ref.shape, F32)

    row = _row_iota((L, L))
    col = _lane_iota((L, L))
    upper = row <= col
    tri = jnp.where(row >= col, 1.0, 0.0).astype(BF16)
    tri_t = jnp.where(upper, 1.0, 0.0).astype(BF16)

    lane1 = _lane_iota((1, LANES))
    head_lane = (lane1 >= M_HEADS) & (lane1 < 2 * M_HEADS)
    lane128 = _lane_iota((L, LANES))
    ones_rows = jnp.ones((ONES_ROWS, L), BF16)
    chains = [(bi, h) for bi in range(k_ref.shape[0]) for h in range(M_HEADS)]

    gate_stats = []
    for bi in range(k_ref.shape[0]):
        gcol = gcol_ref[bi]
        grow = grow_ref[bi]
        ccol = sum(_dot(tri, part) for part in _split3(gcol))
        crow = sum(_dot(part, tri_t) for part in _split3(grow))
        c_all = ccol - pltpu.roll(gcol, M_HEADS, axis=1)
        btot_all = ccol[L - 1:L, :]
        m0_all = m_ref[bi]
        g_all = btot_all - c_all
        m_loc_all = jnp.max(g_all, axis=0, keepdims=True)
        wgt_all = jnp.exp(g_all - m_loc_all)
        m_new_all = jnp.where(head_lane, jnp.maximum(btot_all + m0_all, m_loc_all), 0.0)
        a_all = jnp.exp(btot_all + m0_all - m_new_all)
        e_all = jnp.exp(m_loc_all - m_new_all)
        m_ref[bi] = m_new_all
        gate_stats.append((crow, c_all, m0_all, wgt_all, a_all, e_all))

    def head_keys(bi, h):
        pr, half = divmod(h, 2)
        kp = k_ref[bi, :, pr * LANES:(pr + 1) * LANES]
        in_half = (lane128 >= half * M_DK) & (lane128 < (half + 1) * M_DK)
        return kp, in_half

    sts = []
    for bi, h in chains:
        kp, in_half = head_keys(bi, h)
        kh = jnp.where(in_half, kp, jnp.zeros_like(kp))
        sts.append(_dot(kh, qt_ref[bi, (h // 2) * LANES:(h // 2 + 1) * LANES, :]))

    for (bi, h), st in zip(chains, sts):
        crow, c_all, m0_all, wgt_all, a_all, e_all = gate_stats[bi]
        hl = M_HEADS + h
        bc_row = crow[hl:hl + 1, :]
        m0 = m0_all[:, hl:hl + 1]
        log_d = jnp.where(upper, bc_row - c_all[:, hl:hl + 1], -jnp.inf)
        m_d = jnp.max(log_d, axis=0, keepdims=True)
        log_inter = bc_row + m0
        m_t = jnp.maximum(log_inter, m_d)
        d_m = jnp.exp(log_d - m_t)
        inter = jnp.exp(log_inter - m_t)
        sc = (st * d_m).astype(BF16)
        vaug = jnp.concatenate([vt_ref[bi, h * M_DV:(h + 1) * M_DV, :], ones_rows], axis=0)
        intra = _dot(vaug, sc)
        cn = cn_ref[bi * M_HEADS + h]
        cross = _dot(cn.astype(BF16), qt_ref[bi, (h // 2) * LANES:(h // 2 + 1) * LANES, :])
        num = intra[0:M_DV] + inter * cross[0:M_DV]
        den = intra[M_DV:M_DV + 1] + inter * cross[M_DV:M_DV + 1]
        hh = jnp.transpose(num * (1.0 / jnp.maximum(jnp.abs(den), jnp.exp(-m_t))))
        y = og_ref[bi, :, h * M_DV:(h + 1) * M_DV].astype(F32) * hh
        y = _rms(y, g_ref[:, h * M_DV:(h + 1) * M_DV])
        o_ref[bi, :, h * M_DV:(h + 1) * M_DV] = y.astype(o_ref.dtype)

        kp, in_half = head_keys(bi, h)
        kw = (jnp.where(in_half, kp.astype(F32), 0.0) * wgt_all[:, hl:hl + 1]).astype(BF16)
        d_cn = _dot(vaug, kw)
        cn_ref[bi * M_HEADS + h] = a_all[:, hl:hl + 1] * cn + e_all[:, hl:hl + 1] * d_cn


def _mlstm(qmt, km, vmt, og, gcol, grow, g_out):
    bsz, s, _ = km.shape
    L = M_CHUNK
    nb = M_SEQS
    tok = lambda w: pl.BlockSpec((nb, L, w), lambda b, c: (b, c, 0))
    tok_t = lambda w: pl.BlockSpec((nb, w, L), lambda b, c: (b, 0, c))
    return pl.pallas_call(
        _mlstm_kernel,
        out_shape=jax.ShapeDtypeStruct((bsz, s, M_OUT), BF16),
        grid=(bsz // nb, s // L),
        in_specs=[
            tok_t(M_QK), tok(M_QK), tok_t(M_OUT), tok(M_OUT), tok(LANES), tok_t(SUBLANES),
            pl.BlockSpec((1, M_OUT), lambda b, c: (0, 0)),
        ],
        out_specs=tok(M_OUT),
        scratch_shapes=[pltpu.VMEM((nb * M_HEADS, M_DV + ONES_ROWS, LANES), F32),
                        pltpu.VMEM((nb, 1, LANES), F32)],
        compiler_params=pltpu.CompilerParams(
            dimension_semantics=("parallel", "arbitrary"), vmem_limit_bytes=VMEM_LIMIT),
        name="mlstm",
    )(qmt, km, vmt, og, gcol, grow, g_out)


def _outffn_kernel(x_ref, ya_ref, yb_ref, mod_ref, wout_ref, gffn_ref, wg_ref, wu_ref, wd_ref, gfin_ref,
                   o_ref, *, final_norm):
    gate_a = mod_ref[0, 2:3, :]
    shift_f = mod_ref[0, 3:4, :]
    gain_f = gffn_ref[...] * (1.0 + mod_ref[0, 4:5, :])
    gate_f = mod_ref[0, 5:6, :]
    d_ff = wg_ref.shape[1]
    step = -(-d_ff // FF_CHUNKS // (2 * LANES)) * (2 * LANES)

    for r0 in range(0, x_ref.shape[1], TM_OUT_SUB):
        rows = slice(r0, r0 + TM_OUT_SUB)
        mix = (_dot(ya_ref[0, rows, :], wout_ref[0:MLA_OUT, :])
               + _dot(yb_ref[0, rows, :], wout_ref[MLA_OUT:MLA_OUT + M_OUT, :]))
        x1 = x_ref[0, rows, :] + gate_a * mix
        hb = (_rms(x1, gain_f) + shift_f).astype(BF16)
        ffn = None
        for c0 in range(0, d_ff, step):
            c1 = min(c0 + step, d_ff)
            gt = _dot(hb, wg_ref[:, c0:c1])
            up = _dot(hb, wu_ref[:, c0:c1])
            act = (gt * _sigmoid(gt) * up).astype(BF16)
            part = _dot(act, wd_ref[c0:c1, :])
            ffn = part if ffn is None else ffn + part
        x2 = x1 + gate_f * ffn
        if final_norm:
            x2 = _rms(x2, gfin_ref[...])
        o_ref[0, rows, :] = x2


def _outffn(x, ya, yb, mod, wout, gffn, wg, wu, wd, gfin, final_norm):
    bsz, s, d = x.shape
    tm = TM_OUT
    tok = lambda w: pl.BlockSpec((1, tm, w), lambda b, i: (b, i, 0))
    return pl.pallas_call(
        functools.partial(_outffn_kernel, final_norm=final_norm),
        out_shape=jax.ShapeDtypeStruct((bsz, s, d), F32),
        grid=(bsz, s // tm),
        in_specs=[
            tok(d), tok(MLA_OUT), tok(M_OUT),
            pl.BlockSpec((1, 6, d), lambda b, i: (b, 0, 0)),
            _const_spec(wout.shape),
            _const_spec((1, d)),
            _const_spec(wg.shape), _const_spec(wu.shape), _const_spec(wd.shape),
            _const_spec((1, d)),
        ],
        out_specs=tok(d),
        compiler_params=pltpu.CompilerParams(
            dimension_semantics=("parallel", "parallel"), vmem_limit_bytes=VMEM_LIMIT),
        name="outproj_ffn",
    )(x, ya, yb, mod, wout, gffn, wg, wu, wd, gfin)


def _prep_w_in(w_in):
    d = w_in.shape[0]
    o_q, o_kv, o_kr = 0, MLA_Q_RANK, MLA_Q_RANK + MLA_KV_RANK
    o_qk = o_kr + MLA_ROPE
    o_v = o_qk + 2 * M_QK
    o_o = o_v + M_OUT
    o_i = o_o + M_OUT
    o_f = o_i + M_HEADS
    small = jnp.concatenate([
        w_in[:, o_i:o_i + M_HEADS], w_in[:, o_f:o_f + M_HEADS],
        jnp.zeros((d, ROPE_LO - 2 * M_HEADS), w_in.dtype),
        w_in[:, o_kr:o_kr + MLA_ROPE],
        jnp.zeros((d, HEAD_PAD - ROPE_HI), w_in.dtype)], axis=1)
    return jnp.concatenate([w_in[:, o_q:o_kr], w_in[:, o_qk:o_i], small], axis=1).astype(BF16)


def _prep_w_uq(w_uq):
    r = w_uq.shape[0]
    w = w_uq.reshape(r, MLA_HEADS, MLA_NOPE + MLA_ROPE)
    w = jnp.pad(w, ((0, 0), (0, 0), (0, HEAD_PAD - MLA_NOPE - MLA_ROPE)))
    return w.reshape(r, QK_PAD).astype(BF16)


def _prep_w_ukv(w_ukv):
    r = w_ukv.shape[0]
    w = w_ukv.reshape(r, MLA_HEADS, MLA_NOPE + MLA_V)
    wk = jnp.pad(w[:, :, :MLA_NOPE], ((0, 0), (0, 0), (0, HEAD_PAD - MLA_NOPE))).reshape(r, QK_PAD)
    wv = w[:, :, MLA_NOPE:].reshape(r, MLA_OUT)
    return wk.astype(BF16), wv.astype(BF16)


def kernel(x, c, positions, w_ada, b_ada, g_mix, w_in, g_q, w_uq, g_kv, w_ukv, conv_w, conv_b, b_gates,
           g_out_mla, g_out_mlstm, w_out, g_ffn, w_gate, w_up, w_down, g_final):
    bsz, s, d = x.shape
    depth = w_ada.shape[0]
    cos_blk, sin_blk = _rope_tables(positions)
    row = lambda v: v.reshape(1, -1)
    for l in range(depth):
        mod = _adaln(c, w_ada[l], b_ada[l]).reshape(bsz, 6, d)
        wuk_p, wv_p = _prep_w_ukv(w_ukv[l])
        bg_blk = jnp.pad(b_gates[l], (0, LANES - 2 * M_HEADS)).reshape(1, LANES)
        qa, ka, vt, qmt, km, vmt, og, gcol, grow = _inproj(
            x, mod, row(g_mix[l]), _prep_w_in(w_in[l]), row(g_q[l]), _prep_w_uq(w_uq[l]),
            row(g_kv[l]), wuk_p, wv_p, conv_w[l], row(conv_b[l]), bg_blk, cos_blk, sin_blk)
        ya = _mla(qa, ka, vt, row(g_out_mla[l]))
        yb = _mlstm(qmt, km, vmt, og, gcol, grow, row(g_out_mlstm[l]))
        x = _outffn(x, ya, yb, mod, w_out[l].astype(BF16), row(g_ffn[l]),
                    w_gate[l].astype(BF16), w_up[l].astype(BF16), w_down[l].astype(BF16),
                    row(g_final), final_norm=(l == depth - 1))
    return x
```

```python
import functools

import numpy as np
import jax
import jax.numpy as jnp
from jax import lax
from jax.experimental import pallas as pl
from jax.experimental.pallas import tpu as pltpu

F32 = jnp.float32
BF16 = jnp.bfloat16

LANES = 128
SUBLANES = 8

MLA_HEADS = 8
MLA_NOPE = 64
MLA_ROPE = 32
MLA_V = 64
MLA_Q_RANK = 384
MLA_KV_RANK = 256
ROPE_THETA = 10000.0
ROPE_HALF = MLA_ROPE // 2
M_HEADS = 4
M_DK = 64
M_DV = 128
CONV_W = 4
EPS = 1e-6
LOG2E = 1.4426950408889634

MLA_OUT = MLA_HEADS * MLA_V
M_OUT = M_HEADS * M_DV
M_QK = M_HEADS * M_DK
HEAD_PAD = LANES
QK_PAD = MLA_HEADS * HEAD_PAD
ROPE_LO = MLA_NOPE
ROPE_MID = MLA_NOPE + ROPE_HALF
ROPE_HI = MLA_NOPE + MLA_ROPE

C_Q = 0
C_KV = C_Q + MLA_Q_RANK
C_QK = C_KV + MLA_KV_RANK
C_V = C_QK + 2 * M_QK
C_O = C_V + M_OUT
C_S = C_O + M_OUT
D_IN_PAD = C_S + LANES

TM_ROPE = 1024
TM_IN = 1024
TM_SUB = 512
TQ = 512
MLA_HPS = 4
MLA_QG = 2
ONES_ROWS = 16
M_CHUNK = 256
M_SEQS = 1
TM_OUT = 1024
TM_OUT_SUB = 512
FF_CHUNKS = 3

VMEM_LIMIT = 56 * 1024 * 1024


def _lane_iota(shape):
    return lax.broadcasted_iota(jnp.int32, shape, len(shape) - 1)


def _row_iota(shape):
    return lax.broadcasted_iota(jnp.int32, shape, len(shape) - 2)


def _dot(a, b):
    return jnp.dot(a, b, preferred_element_type=F32)


def _dot_nt(a, b):
    return lax.dot_general(a, b, (((1,), (1,)), ((), ())), preferred_element_type=F32)


def _log_sigmoid(x):
    return jnp.minimum(x, 0.0) - jnp.log1p(jnp.exp(-jnp.abs(x)))


def _sigmoid(x):
    return 1.0 / (1.0 + jnp.exp(-x))


def _adaln_kernel(c_ref, w_ref, b_ref, o_ref):
    c = c_ref[...]
    cond = c * _sigmoid(c)
    o_ref[...] = _dot(cond, w_ref[...]) + b_ref[...]


def _adaln(c, w, b):
    bsz, d = c.shape
    n = w.shape[1]
    tn = 1536
    return pl.pallas_call(
        _adaln_kernel,
        out_shape=jax.ShapeDtypeStruct((bsz, n), F32),
        grid=(n // tn,),
        in_specs=[
            pl.BlockSpec((bsz, d), lambda j: (0, 0)),
            pl.BlockSpec((d, tn), lambda j: (0, j)),
            pl.BlockSpec((1, tn), lambda j: (0, j)),
        ],
        out_specs=pl.BlockSpec((bsz, tn), lambda j: (0, j)),
        compiler_params=pltpu.CompilerParams(
            dimension_semantics=("parallel",), vmem_limit_bytes=VMEM_LIMIT),
        name="adaln",
    )(c, w, b.reshape(1, n))


def _rope_kernel(pos_ref, inv_ref, cos_ref, sin_ref):
    ang = inv_ref[...] * pos_ref[0].astype(F32)
    c = jnp.cos(ang)
    sn = jnp.sin(ang)
    t = ang.shape[1]
    lo = jnp.ones((ROPE_LO, t), F32)
    hi = jnp.ones((HEAD_PAD - ROPE_HI, t), F32)
    cos_ref[0] = jnp.transpose(jnp.concatenate([lo, c, c, hi], axis=0))
    sin_ref[0] = jnp.transpose(jnp.concatenate([0.0 * lo, -sn, sn, 0.0 * hi], axis=0))


def _rope_tables(positions):
    bsz, s = positions.shape
    t = TM_ROPE
    inv = ROPE_THETA ** (-np.arange(ROPE_HALF, dtype=np.float64) / ROPE_HALF)
    inv = jnp.asarray(inv.astype(np.float32).reshape(ROPE_HALF, 1))
    out = jax.ShapeDtypeStruct((bsz, s, HEAD_PAD), F32)
    blk = pl.BlockSpec((1, t, HEAD_PAD), lambda b, i: (b, i, 0))
    return pl.pallas_call(
        _rope_kernel,
        out_shape=(out, out),
        grid=(bsz, s // t),
        in_specs=[
            pl.BlockSpec((1, 1, t), lambda b, i: (b, 0, i)),
            pl.BlockSpec((ROPE_HALF, 1), lambda b, i: (0, 0)),
        ],
        out_specs=(blk, blk),
        compiler_params=pltpu.CompilerParams(
            dimension_semantics=("parallel", "parallel"), vmem_limit_bytes=VMEM_LIMIT),
        name="rope_tables",
    )(positions.reshape(bsz, 1, s), inv)


def _rms(x, g):
    return x * lax.rsqrt(jnp.mean(x * x, axis=-1, keepdims=True) + EPS) * g


def _rope_block(x, cos_blk, sin_blk, lane):
    fwd = pltpu.roll(x, HEAD_PAD - ROPE_HALF, axis=1)
    bwd = pltpu.roll(x, ROPE_HALF, axis=1)
    rot = jnp.where(lane < ROPE_MID, fwd, bwd)
    return x * cos_blk + rot * sin_blk


def _inproj_kernel(x_ref, mod_ref, gmix_ref, win_ref, gq_ref, wuq_ref, gkv_ref, wuk_ref, wv_ref,
                   convw_ref, convb_ref, bg_ref, cos_ref, sin_ref,
                   qa_ref, ka_ref, vt_ref, qmt_ref, km_ref, vmt_ref, og_ref, gcol_ref, grow_ref,
                   zbuf_ref):
    tm = x_ref.shape[1]
    si = pl.program_id(1)
    shift = mod_ref[0, 0:1, :]
    gain = gmix_ref[...] * (1.0 + mod_ref[0, 1:2, :])
    lane = _lane_iota((TM_SUB, LANES))

    @pl.when(si == 0)
    def _():
        zbuf_ref[0:SUBLANES, :] = jnp.zeros((SUBLANES, 2 * M_QK), F32)

    for r0 in range(0, tm, TM_SUB):
        rows = slice(r0, r0 + TM_SUB)
        hb = (_rms(x_ref[0, rows, :], gain) + shift).astype(BF16)
        cos_blk = cos_ref[0, rows, :]
        sin_blk = sin_ref[0, rows, :]

        ql = _dot(hb, win_ref[:, C_Q:C_KV])
        qn = _rms(ql, gq_ref[...]).astype(BF16)
        qa = _dot(qn, wuq_ref[...]) * ((MLA_NOPE + MLA_ROPE) ** -0.5 * LOG2E)
        for hd in range(MLA_HEADS):
            blk = qa[:, hd * HEAD_PAD:(hd + 1) * HEAD_PAD]
            qa_ref[0, rows, hd * HEAD_PAD:(hd + 1) * HEAD_PAD] = (
                _rope_block(blk, cos_blk, sin_blk, lane).astype(BF16))

        zs = _dot(hb, win_ref[:, C_S:D_IN_PAD])
        kr = _rope_block(zs, cos_blk, sin_blk, lane)
        kr = jnp.where((lane >= ROPE_LO) & (lane < ROPE_HI), kr, 0.0)
        gpre = zs + bg_ref[...]
        gates = jnp.where(lane < M_HEADS, gpre, _log_sigmoid(gpre))
        gcol_ref[0, rows, :] = gates
        grow_ref[0, :, rows] = jnp.transpose(gates)[0:SUBLANES, :]

        kl = _dot(hb, win_ref[:, C_KV:C_QK])
        kn = _rms(kl, gkv_ref[...]).astype(BF16)
        ka = _dot(kn, wuk_ref[...])
        for hd in range(MLA_HEADS):
            ka_ref[0, rows, hd * HEAD_PAD:(hd + 1) * HEAD_PAD] = (
                ka[:, hd * HEAD_PAD:(hd + 1) * HEAD_PAD] + kr).astype(BF16)
        vt_ref[0, :, rows] = jnp.transpose(_dot(kn, wv_ref[...])).astype(BF16)

        zqk = _dot(hb, win_ref[:, C_QK:C_V])
        z0 = SUBLANES + r0
        zbuf_ref[z0:z0 + TM_SUB, :] = zqk
        acc = zqk * convw_ref[CONV_W - 1:CONV_W, :] + convb_ref[...]
        for j in range(1, CONV_W):
            acc = acc + zbuf_ref[z0 - j:z0 - j + TM_SUB, :] * convw_ref[CONV_W - 1 - j:CONV_W - j, :]
        qk = acc * _sigmoid(acc)
        qmt_ref[0, :, rows] = jnp.transpose(qk[:, 0:M_QK]).astype(BF16)
        km_ref[0, rows, :] = (qk[:, M_QK:2 * M_QK] * (M_DK ** -0.5)).astype(BF16)

        vmt_ref[0, :, rows] = jnp.transpose(_dot(hb, win_ref[:, C_V:C_O])).astype(BF16)
        og_ref[0, rows, :] = _sigmoid(_dot(hb, win_ref[:, C_O:C_S])).astype(BF16)

    zbuf_ref[0:SUBLANES, :] = zbuf_ref[tm:tm + SUBLANES, :]


def _const_spec(shape):
    nd = len(shape)
    return pl.BlockSpec(shape, lambda *_: (0,) * nd, pipeline_mode=pl.Buffered(1))


def _inproj(x, mod, gmix, win_p, gq, wuq_p, gkv, wuk_p, wv_p, convw, convb, bg_blk, cos_blk, sin_blk):
    bsz, s, d = x.shape
    tm = TM_IN
    tok = lambda w: pl.BlockSpec((1, tm, w), lambda b, i: (b, i, 0))
    out_shapes = (
        jax.ShapeDtypeStruct((bsz, s, QK_PAD), BF16),
        jax.ShapeDtypeStruct((bsz, s, QK_PAD), BF16),
        jax.ShapeDtypeStruct((bsz, MLA_OUT, s), BF16),
        jax.ShapeDtypeStruct((bsz, M_QK, s), BF16),
        jax.ShapeDtypeStruct((bsz, s, M_QK), BF16),
        jax.ShapeDtypeStruct((bsz, M_OUT, s), BF16),
        jax.ShapeDtypeStruct((bsz, s, M_OUT), BF16),
        jax.ShapeDtypeStruct((bsz, s, LANES), F32),
        jax.ShapeDtypeStruct((bsz, SUBLANES, s), F32),
    )
    out_specs = (
        tok(QK_PAD), tok(QK_PAD),
        pl.BlockSpec((1, MLA_OUT, tm), lambda b, i: (b, 0, i)),
        pl.BlockSpec((1, M_QK, tm), lambda b, i: (b, 0, i)),
        tok(M_QK),
        pl.BlockSpec((1, M_OUT, tm), lambda b, i: (b, 0, i)),
        tok(M_OUT), tok(LANES),
        pl.BlockSpec((1, SUBLANES, tm), lambda b, i: (b, 0, i)),
    )
    in_specs = [
        tok(d),
        pl.BlockSpec((1, 6, d), lambda b, i: (b, 0, 0)),
        _const_spec((1, d)),
        _const_spec(win_p.shape),
        _const_spec((1, MLA_Q_RANK)),
        _const_spec(wuq_p.shape),
        _const_spec((1, MLA_KV_RANK)),
        _const_spec(wuk_p.shape),
        _const_spec(wv_p.shape),
        _const_spec((CONV_W, 2 * M_QK)),
        _const_spec((1, 2 * M_QK)),
        _const_spec((1, LANES)),
        tok(LANES), tok(LANES),
    ]
    return pl.pallas_call(
        _inproj_kernel,
        out_shape=out_shapes,
        grid=(bsz, s // tm),
        in_specs=in_specs,
        out_specs=out_specs,
        scratch_shapes=[pltpu.VMEM((tm + SUBLANES, 2 * M_QK), F32)],
        compiler_params=pltpu.CompilerParams(
            dimension_semantics=("parallel", "arbitrary"), vmem_limit_bytes=VMEM_LIMIT),
        name="inproj",
    )(x, mod, gmix, win_p, gq, wuq_p, gkv, wuk_p, wv_p, convw, convb, bg_blk, cos_blk, sin_blk)


def _mla_kernel(q_ref, k_ref, vt_ref, g_ref, o_ref, m_ref, acc_ref):
    s = q_ref.shape[1]
    half = TQ // 2
    causal_a = _row_iota((half, TQ)) <= _lane_iota((half, TQ))
    causal_b = _row_iota((half, half)) <= _lane_iota((half, half))
    ones_rows = jnp.ones((ONES_ROWS, TQ), BF16)

    def q_group(a, carry):
        q0s = [pl.multiple_of((MLA_QG * a + t) * TQ, TQ) for t in range(MLA_QG)]
        qs = [[q_ref[0, pl.ds(q0s[t], TQ), h * HEAD_PAD:(h + 1) * HEAD_PAD] for h in range(MLA_HPS)]
              for t in range(MLA_QG)]
        for c in range(MLA_QG * MLA_HPS):
            m_ref[c] = jnp.full((1, TQ), -jnp.inf, F32)
            acc_ref[c] = jnp.zeros((MLA_V + ONES_ROWS, TQ), F32)

        def kv_step(k0, tiles):
            chains = [(t, h) for t in tiles for h in range(MLA_HPS)]
            ks = [k_ref[0, pl.ds(k0, TQ), h * HEAD_PAD:(h + 1) * HEAD_PAD] for h in range(MLA_HPS)]
            sts = [_dot_nt(ks[h], qs[t][h]) for t, h in chains]
            upd = []
            for (t, h), st in zip(chains, sts):
                c = t * MLA_HPS + h
                m = m_ref[c]
                m_new = jnp.maximum(m, jnp.max(st, axis=0, keepdims=True))
                p = jnp.exp2(st - m_new).astype(BF16)
                vt = jnp.concatenate([vt_ref[0, h * MLA_V:(h + 1) * MLA_V, pl.ds(k0, TQ)], ones_rows], axis=0)
                upd.append((c, m_new, jnp.exp2(m - m_new), _dot(vt, p)))
            for c, m_new, alpha, pv in upd:
                m_ref[c] = m_new
                acc_ref[c] = alpha * acc_ref[c] + pv

        def diag_step(t):
            k0 = q0s[t]
            k1 = pl.multiple_of(k0 + half, half)
            heads = range(MLA_HPS)
            sa = [_dot_nt(k_ref[0, pl.ds(k0, half), h * HEAD_PAD:(h + 1) * HEAD_PAD], qs[t][h]) for h in heads]
            sb = [_dot_nt(k_ref[0, pl.ds(k1, half), h * HEAD_PAD:(h + 1) * HEAD_PAD], qs[t][h][half:, :])
                  for h in heads]
            upd = []
            for h in heads:
                c = t * MLA_HPS + h
                st_a = jnp.where(causal_a, sa[h], -jnp.inf)
                st_b = jnp.where(causal_b, sb[h], -jnp.inf)
                m = m_ref[c]
                m_a = jnp.maximum(m, jnp.max(st_a, axis=0, keepdims=True))
                m_b = jnp.maximum(m_a[:, half:], jnp.max(st_b, axis=0, keepdims=True))
                m_new = jnp.concatenate([m_a[:, :half], m_b], axis=1)
                p_a = jnp.exp2(st_a - m_new).astype(BF16)
                p_b = jnp.exp2(st_b - m_b).astype(BF16)
                rows = slice(h * MLA_V, (h + 1) * MLA_V)
                vt_a = jnp.concatenate([vt_ref[0, rows, pl.ds(k0, half)], ones_rows[:, :half]], axis=0)
                vt_b = jnp.concatenate([vt_ref[0, rows, pl.ds(k1, half)], ones_rows[:, :half]], axis=0)
                upd.append((c, m_new, jnp.exp2(m - m_new), _dot(vt_a, p_a), _dot(vt_b, p_b)))
            for c, m_new, alpha, pv_a, pv_b in upd:
                acc = alpha * acc_ref[c] + pv_a
                m_ref[c] = m_new
                acc_ref[c] = jnp.concatenate([acc[:, :half], acc[:, half:] + pv_b], axis=1)

        all_tiles = tuple(range(MLA_QG))

        def body(kj, carry2):
            kv_step(pl.multiple_of(kj * TQ, TQ), all_tiles)
            return carry2

        lax.fori_loop(0, MLA_QG * a, body, 0)
        for t in range(MLA_QG):
            diag_step(t)
            if t + 1 < MLA_QG:
                kv_step(q0s[t], all_tiles[t + 1:])

        for t in range(MLA_QG):
            for pr in range(MLA_HPS // 2):
                halves = []
                for h in (2 * pr, 2 * pr + 1):
                    acc = acc_ref[t * MLA_HPS + h]
                    o = acc[0:MLA_V] * (1.0 / acc[MLA_V:MLA_V + 1])
                    ms = jnp.mean(o * o, axis=0, keepdims=True)
                    halves.append(o * lax.rsqrt(ms + EPS))
                y = jnp.transpose(jnp.concatenate(halves, axis=0))
                y = y * g_ref[:, pr * LANES:(pr + 1) * LANES]
                o_ref[0, pl.ds(q0s[t], TQ), pr * LANES:(pr + 1) * LANES] = y.astype(o_ref.dtype)
        return carry

    lax.fori_loop(0, s // (MLA_QG * TQ), q_group, 0)


def _mla(qa, ka, vt, g_out):
    bsz, s, _ = qa.shape
    groups = MLA_HEADS // MLA_HPS
    return pl.pallas_call(
        _mla_kernel,
        out_shape=jax.ShapeDtypeStruct((bsz, s, MLA_OUT), BF16),
        grid=(bsz, groups),
        in_specs=[
            pl.BlockSpec((1, s, MLA_HPS * HEAD_PAD), lambda b, p: (b, 0, p)),
            pl.BlockSpec((1, s, MLA_HPS * HEAD_PAD), lambda b, p: (b, 0, p)),
            pl.BlockSpec((1, MLA_HPS * MLA_V, s), lambda b, p: (b, p, 0)),
            pl.BlockSpec((1, MLA_HPS * MLA_V), lambda b, p: (0, p)),
        ],
        out_specs=pl.BlockSpec((1, s, MLA_HPS * MLA_V), lambda b, p: (b, 0, p)),
        scratch_shapes=[pltpu.VMEM((MLA_QG * MLA_HPS, 1, TQ), F32),
                        pltpu.VMEM((MLA_QG * MLA_HPS, MLA_V + ONES_ROWS, TQ), F32)],
        compiler_params=pltpu.CompilerParams(
            dimension_semantics=("parallel", "parallel"), vmem_limit_bytes=VMEM_LIMIT),
        name="mla_attention",
    )(qa, ka, vt, g_out)


def _split3(x):
    hi = x.astype(BF16)
    r1 = x - hi.astype(F32)
    mid = r1.astype(BF16)
    lo = (r1 - mid.astype(F32)).astype(BF16)
    return hi, mid, lo


def _mlstm_kernel(qt_ref, k_ref, vt_ref, og_ref, gcol_ref, grow_ref, g_ref, o_ref, cn_ref, m_ref):
    L = k_ref.shape[1]
    ci = pl.program_id(1)

    @pl.when(ci == 0)
    def _():
        cn_ref[...] = jnp.zeros(cn_ref.shape, F32)
        m_ref[...] = jnp.zeros(m_ref.shape, F32)

    row = _row_iota((L, L))
    col = _lane_iota((L, L))
    upper = row <= col
    tri = jnp.where(row >= col, 1.0, 0.0).astype(BF16)
    tri_t = jnp.where(upper, 1.0, 0.0).astype(BF16)

    lane1 = _lane_iota((1, LANES))
    head_lane = (lane1 >= M_HEADS) & (lane1 < 2 * M_HEADS)
    lane128 = _lane_iota((L, LANES))
    ones_rows = jnp.ones((ONES_ROWS, L), BF16)
    chains = [(bi, h) for bi in range(k_ref.shape[0]) for h in range(M_HEADS)]

    gate_stats = []
    for bi in range(k_ref.shape[0]):
        gcol = gcol_ref[bi]
        grow = grow_ref[bi]
        ccol = sum(_dot(tri, part) for part in _split3(gcol))
        crow = sum(_dot(part, tri_t) for part in _split3(grow))
        c_all = ccol - pltpu.roll(gcol, M_HEADS, axis=1)
        btot_all = ccol[L - 1:L, :]
        m0_all = m_ref[bi]
        g_all = btot_all - c_all
        m_loc_all = jnp.max(g_all, axis=0, keepdims=True)
        wgt_all = jnp.exp(g_all - m_loc_all)
        m_new_all = jnp.where(head_lane, jnp.maximum(btot_all + m0_all, m_loc_all), 0.0)
        a_all = jnp.exp(btot_all + m0_all - m_new_all)
        e_all = jnp.exp(m_loc_all - m_new_all)
        m_ref[bi] = m_new_all
        gate_stats.append((crow, c_all, m0_all, wgt_all, a_all, e_all))

    def head_keys(bi, h):
        pr, half = divmod(h, 2)
        kp = k_ref[bi, :, pr * LANES:(pr + 1) * LANES]
        in_half = (lane128 >= half * M_DK) & (lane128 < (half + 1) * M_DK)
        return kp, in_half

    sts = []
    for bi, h in chains:
        kp, in_half = head_keys(bi, h)
        kh = jnp.where(in_half, kp, jnp.zeros_like(kp))
        sts.append(_dot(kh, qt_ref[bi, (h // 2) * LANES:(h // 2 + 1) * LANES, :]))

    for (bi, h), st in zip(chains, sts):
        crow, c_all, m0_all, wgt_all, a_all, e_all = gate_stats[bi]
        hl = M_HEADS + h
        bc_row = crow[hl:hl + 1, :]
        m0 = m0_all[:, hl:hl + 1]
        log_d = jnp.where(upper, bc_row - c_all[:, hl:hl + 1], -jnp.inf)
        m_d = jnp.max(log_d, axis=0, keepdims=True)
        log_inter = bc_row + m0
        m_t = jnp.maximum(log_inter, m_d)
        d_m = jnp.exp(log_d - m_t)
        inter = jnp.exp(log_inter - m_t)
        sc = (st * d_m).astype(BF16)
        vaug = jnp.concatenate([vt_ref[bi, h * M_DV:(h + 1) * M_DV, :], ones_rows], axis=0)
        intra = _dot(vaug, sc)
        cn = cn_ref[bi * M_HEADS + h]
        cross = _dot(cn.astype(BF16), qt_ref[bi, (h // 2) * LANES:(h // 2 + 1) * LANES, :])
        num = intra[0:M_DV] + inter * cross[0:M_DV]
        den = intra[M_DV:M_DV + 1] + inter * cross[M_DV:M_DV + 1]
        hh = jnp.transpose(num * (1.0 / jnp.maximum(jnp.abs(den), jnp.exp(-m_t))))
        y = og_ref[bi, :, h * M_DV:(h + 1) * M_DV].astype(F32) * hh
        y = _rms(y, g_ref[:, h * M_DV:(h + 1) * M_DV])
        o_ref[bi, :, h * M_DV:(h + 1) * M_DV] = y.astype(o_ref.dtype)

        kp, in_half = head_keys(bi, h)
        kw = (jnp.where(in_half, kp.astype(F32), 0.0) * wgt_all[:, hl:hl + 1]).astype(BF16)
        d_cn = _dot(vaug, kw)
        cn_ref[bi * M_HEADS + h] = a_all[:, hl:hl + 1] * cn + e_all[:, hl:hl + 1] * d_cn


def _mlstm(qmt, km, vmt, og, gcol, grow, g_out):
    bsz, s, _ = km.shape
    L = M_CHUNK
    nb = M_SEQS
    tok = lambda w: pl.BlockSpec((nb, L, w), lambda b, c: (b, c, 0))
    tok_t = lambda w: pl.BlockSpec((nb, w, L), lambda b, c: (b, 0, c))
    return pl.pallas_call(
        _mlstm_kernel,
        out_shape=jax.ShapeDtypeStruct((bsz, s, M_OUT), BF16),
        grid=(bsz // nb, s // L),
        in_specs=[
            tok_t(M_QK), tok(M_QK), tok_t(M_OUT), tok(M_OUT), tok(LANES), tok_t(SUBLANES),
            pl.BlockSpec((1, M_OUT), lambda b, c: (0, 0)),
        ],
        out_specs=tok(M_OUT),
        scratch_shapes=[pltpu.VMEM((nb * M_HEADS, M_DV + ONES_ROWS, LANES), F32),
                        pltpu.VMEM((nb, 1, LANES), F32)],
        compiler_params=pltpu.CompilerParams(
            dimension_semantics=("parallel", "arbitrary"), vmem_limit_bytes=VMEM_LIMIT),
        name="mlstm",
    )(qmt, km, vmt, og, gcol, grow, g_out)


def _outffn_kernel(x_ref, ya_ref, yb_ref, mod_ref, wout_ref, gffn_ref, wg_ref, wu_ref, wd_ref, gfin_ref,
                   o_ref, *, final_norm):
    gate_a = mod_ref[0, 2:3, :]
    shift_f = mod_ref[0, 3:4, :]
    gain_f = gffn_ref[...] * (1.0 + mod_ref[0, 4:5, :])
    gate_f = mod_ref[0, 5:6, :]
    d_ff = wg_ref.shape[1]
    step = -(-d_ff // FF_CHUNKS // (2 * LANES)) * (2 * LANES)

    for r0 in range(0, x_ref.shape[1], TM_OUT_SUB):
        rows = slice(r0, r0 + TM_OUT_SUB)
        mix = (_dot(ya_ref[0, rows, :], wout_ref[0:MLA_OUT, :])
               + _dot(yb_ref[0, rows, :], wout_ref[MLA_OUT:MLA_OUT + M_OUT, :]))
        x1 = x_ref[0, rows, :] + gate_a * mix
        hb = (_rms(x1, gain_f) + shift_f).astype(BF16)
        ffn = None
        for c0 in range(0, d_ff, step):
            c1 = min(c0 + step, d_ff)
            gt = _dot(hb, wg_ref[:, c0:c1])
            up = _dot(hb, wu_ref[:, c0:c1])
            act = (gt * _sigmoid(gt) * up).astype(BF16)
            part = _dot(act, wd_ref[c0:c1, :])
            ffn = part if ffn is None else ffn + part
        x2 = x1 + gate_f * ffn
        if final_norm:
            x2 = _rms(x2, gfin_ref[...])
        o_ref[0, rows, :] = x2


def _outffn(x, ya, yb, mod, wout, gffn, wg, wu, wd, gfin, final_norm):
    bsz, s, d = x.shape
    tm = TM_OUT
    tok = lambda w: pl.BlockSpec((1, tm, w), lambda b, i: (b, i, 0))
    return pl.pallas_call(
        functools.partial(_outffn_kernel, final_norm=final_norm),
        out_shape=jax.ShapeDtypeStruct((bsz, s, d), F32),
        grid=(bsz, s // tm),
        in_specs=[
            tok(d), tok(MLA_OUT), tok(M_OUT),
            pl.BlockSpec((1, 6, d), lambda b, i: (b, 0, 0)),
            _const_spec(wout.shape),
            _const_spec((1, d)),
            _const_spec(wg.shape), _const_spec(wu.shape), _const_spec(wd.shape),
            _const_spec((1, d)),
        ],
        out_specs=tok(d),
        compiler_params=pltpu.CompilerParams(
            dimension_semantics=("parallel", "parallel"), vmem_limit_bytes=VMEM_LIMIT),
        name="outproj_ffn",
    )(x, ya, yb, mod, wout, gffn, wg, wu, wd, gfin)


def _prep_w_in(w_in):
    w_in = w_in.astype(BF16)
    d = w_in.shape[0]
    o_q, o_kv, o_kr = 0, MLA_Q_RANK, MLA_Q_RANK + MLA_KV_RANK
    o_qk = o_kr + MLA_ROPE
    o_v = o_qk + 2 * M_QK
    o_o = o_v + M_OUT
    o_i = o_o + M_OUT
    o_f = o_i + M_HEADS
    small = jnp.concatenate([
        w_in[:, o_i:o_i + M_HEADS], w_in[:, o_f:o_f + M_HEADS],
        jnp.zeros((d, ROPE_LO - 2 * M_HEADS), w_in.dtype),
        w_in[:, o_kr:o_kr + MLA_ROPE],
        jnp.zeros((d, HEAD_PAD - ROPE_HI), w_in.dtype)], axis=1)
    return jnp.concatenate([w_in[:, o_q:o_kr], w_in[:, o_qk:o_i], small], axis=1)


def _prep_w_uq(w_uq):
    r = w_uq.shape[0]
    w = w_uq.astype(BF16).reshape(r, MLA_HEADS, MLA_NOPE + MLA_ROPE)
    w = jnp.pad(w, ((0, 0), (0, 0), (0, HEAD_PAD - MLA_NOPE - MLA_ROPE)))
    return w.reshape(r, QK_PAD)


def _prep_w_ukv(w_ukv):
    r = w_ukv.shape[0]
    w = w_ukv.astype(BF16).reshape(r, MLA_HEADS, MLA_NOPE + MLA_V)
    wk = jnp.pad(w[:, :, :MLA_NOPE], ((0, 0), (0, 0), (0, HEAD_PAD - MLA_NOPE))).reshape(r, QK_PAD)
    wv = w[:, :, MLA_NOPE:].reshape(r, MLA_OUT)
    return wk, wv


def kernel(x, c, positions, w_ada, b_ada, g_mix, w_in, g_q, w_uq, g_kv, w_ukv, conv_w, conv_b, b_gates,
           g_out_mla, g_out_mlstm, w_out, g_ffn, w_gate, w_up, w_down, g_final):
    bsz, s, d = x.shape
    depth = w_ada.shape[0]
    cos_blk, sin_blk = _rope_tables(positions)
    row = lambda v: v.reshape(1, -1)
    for l in range(depth):
        mod = _adaln(c, w_ada[l], b_ada[l]).reshape(bsz, 6, d)
        wuk_p, wv_p = _prep_w_ukv(w_ukv[l])
        bg_blk = jnp.pad(b_gates[l], (0, LANES - 2 * M_HEADS)).reshape(1, LANES)
        qa, ka, vt, qmt, km, vmt, og, gcol, grow = _inproj(
            x, mod, row(g_mix[l]), _prep_w_in(w_in[l]), row(g_q[l]), _prep_w_uq(w_uq[l]),
            row(g_kv[l]), wuk_p, wv_p, conv_w[l], row(conv_b[l]), bg_blk, cos_blk, sin_blk)
        ya = _mla(qa, ka, vt, row(g_out_mla[l]))
        yb = _mlstm(qmt, km, vmt, og, gcol, grow, row(g_out_mlstm[l]))
        x = _outffn(x, ya, yb, mod, w_out[l].astype(BF16), row(g_ffn[l]),
                    w_gate[l].astype(BF16), w_up[l].astype(BF16), w_down[l].astype(BF16),
                    row(g_final), final_norm=(l == depth - 1))
    return x
```
